```python
import math
import jax, jax.numpy as jnp
from jax import lax
import numpy as np

D_MODEL = 1024
BATCH = 4
SEQ = 8192
DEPTH = 4

PLE_DIM = 256
N_BRANCH = 4
BRANCH_W = D_MODEL // N_BRANCH
N_IN_BLOCKS = 10
CONV_A_WIDTH = 3
ATT_HEADS = 4
ATT_HEAD_DIM = BRANCH_W // ATT_HEADS
DSW_GROUPS = ((128, 1), (512, 4), (2048, 16))
SGU_CHUNK = 128
SGU_GROUPS = 4
CONF_KERNEL = 31
FFN_HIDDEN = -(-8 * D_MODEL // (3 * 256)) * 256
EPS = 1e-6

kernel_name = 'hybrid_parallel_gated_mixers'


def rmsnorm(x, g):
    xf = x.astype(jnp.float32)
    y = xf * lax.rsqrt(jnp.mean(xf * xf, axis=-1, keepdims=True) + EPS)
    return (y * g.astype(jnp.float32)).astype(x.dtype)


def layernorm(x, g, b):
    xf = x.astype(jnp.float32)
    mu = jnp.mean(xf, axis=-1, keepdims=True)
    var = jnp.mean(jnp.square(xf - mu), axis=-1, keepdims=True)
    y = (xf - mu) * lax.rsqrt(var + EPS)
    return (y * g.astype(jnp.float32) + b.astype(jnp.float32)).astype(x.dtype)


def causal_dwconv(x, w):
    k = w.shape[0]
    return lax.conv_general_dilated(
        x, w[:, None, :].astype(x.dtype), window_strides=(1,), padding=[(k - 1, 0)],
        dimension_numbers=('NWC', 'WIO', 'NWC'), feature_group_count=x.shape[-1])


def dilated_window_group(q, k, v, window, dilation):
    bsz, nh, s, dh = q.shape
    blk = window // dilation
    span = blk * dilation
    sp = -(-s // span) * span
    nb = sp // span

    def to_blocks(t):
        t = jnp.pad(t, ((0, 0), (0, 0), (0, sp - s), (0, 0)))
        t = t.reshape(bsz, nh, sp // dilation, dilation, dh).transpose(0, 1, 3, 2, 4)
        return t.reshape(bsz, nh, dilation, nb, blk, dh)

    def with_prev(t):
        prev = jnp.pad(t, ((0, 0), (0, 0), (0, 0), (1, 0), (0, 0), (0, 0)))[:, :, :, :-1]
        return jnp.concatenate([prev, t], axis=-2)

    qb = to_blocks(q)
    kc = with_prev(to_blocks(k))
    vc = with_prev(to_blocks(v))
    scores = jnp.einsum('bhrnqc,bhrnkc->bhrnqk', qb, kc).astype(jnp.float32) * (dh ** -0.5)
    qi = jnp.arange(blk)[:, None]
    ki = jnp.arange(2 * blk)[None, :]
    dist = qi + blk - ki
    band = (dist >= 0) & (dist <= blk)
    not_before_start = (jnp.arange(nb) > 0)[:, None, None] | (ki >= blk)[None]
    mask = band[None] & not_before_start
    scores = jnp.where(mask, scores, -jnp.inf)
    m = jnp.max(scores, axis=-1, keepdims=True)
    e = jnp.exp(scores - m)
    l = jnp.sum(e, axis=-1, keepdims=True)
    o = jnp.einsum('bhrnqk,bhrnkc->bhrnqc', e, vc.astype(jnp.float32)) / l
    lse = m + jnp.log(l)

    def from_blocks(t):
        c = t.shape[-1]
        t = t.reshape(bsz, nh, dilation, sp // dilation, c).transpose(0, 1, 3, 2, 4)
        return t.reshape(bsz, nh, sp, c)[:, :, :s]

    return from_blocks(o), from_blocks(lse)[..., 0]


def dilated_attention(q, k, v):
    bsz, s, _ = q.shape
    heads = lambda t: t.reshape(bsz, s, ATT_HEADS, ATT_HEAD_DIM).transpose(0, 2, 1, 3)
    qh, kh, vh = heads(q), heads(k), heads(v)
    outs, lses = [], []
    for window, dilation in DSW_GROUPS:
        o_g, lse_g = dilated_window_group(qh, kh, vh, window, dilation)
        outs.append(o_g)
        lses.append(lse_g)
    wts = jax.nn.softmax(jnp.stack(lses), axis=0)
    o = sum(wts[g][..., None] * outs[g] for g in range(len(DSW_GROUPS)))
    return o.transpose(0, 2, 1, 3).reshape(bsz, s, BRANCH_W).astype(q.dtype)


def spatial_gating(u, v, ln_g, ln_b, w_s, b_s):
    bsz, s, c = v.shape
    v = layernorm(v, ln_g, ln_b)
    vb = v.reshape(bsz, s // SGU_CHUNK, SGU_CHUNK, SGU_GROUPS, c // SGU_GROUPS)
    causal = jnp.tril(jnp.ones((SGU_CHUNK, SGU_CHUNK), dtype=bool))
    w = jnp.where(causal[None], w_s, jnp.zeros_like(w_s))
    mixed = jnp.einsum('gts,bnsgc->bntgc', w, vb) + b_s.T[None, None, :, :, None]
    return u * mixed.reshape(bsz, s, c)


def conformer_conv(val, gate, dw, ln_g, ln_b):
    y = val * jax.nn.sigmoid(gate)
    y = causal_dwconv(y, dw)
    y = layernorm(y, ln_g, ln_b)
    return jax.nn.silu(y)


def setup_inputs(seed: int = 0) -> dict:
    key = jax.random.key(seed)
    ks = jax.random.split(key, 22)
    nrm = lambda k, shape: jax.random.normal(k, shape, jnp.float32)
    res = (2.0 * DEPTH) ** -0.5
    bw = BRANCH_W
    return {
        'x': nrm(ks[0], (BATCH, SEQ, D_MODEL)),
        'p': nrm(ks[1], (DEPTH, BATCH, SEQ, PLE_DIM)),
        'g_mix': 1.0 + 0.02 * nrm(ks[2], (DEPTH, D_MODEL)),
        'w_in': nrm(ks[3], (DEPTH, D_MODEL, N_IN_BLOCKS * bw)) * D_MODEL ** -0.5,
        'conv_a': nrm(ks[4], (DEPTH, CONV_A_WIDTH, bw)) * CONV_A_WIDTH ** -0.5,
        'sgu_ln_g': 1.0 + 0.02 * nrm(ks[5], (DEPTH, bw)),
        'sgu_ln_b': 0.02 * nrm(ks[6], (DEPTH, bw)),
        'sgu_w': nrm(ks[7], (DEPTH, SGU_GROUPS, SGU_CHUNK, SGU_CHUNK)) * SGU_CHUNK ** -0.5,
        'sgu_b': 1.0 + 0.02 * nrm(ks[8], (DEPTH, SGU_GROUPS, SGU_CHUNK)),
        'conf_dw': nrm(ks[9], (DEPTH, CONF_KERNEL, bw)) * CONF_KERNEL ** -0.5,
        'conf_ln_g': 1.0 + 0.02 * nrm(ks[10], (DEPTH, bw)),
        'conf_ln_b': 0.02 * nrm(ks[11], (DEPTH, bw)),
        'w_branch': nrm(ks[12], (DEPTH, N_BRANCH, bw, D_MODEL)) * bw ** -0.5,
        'w_merge_gate': nrm(ks[13], (DEPTH, N_BRANCH, D_MODEL, D_MODEL)) * D_MODEL ** -0.5,
        'w_out': nrm(ks[14], (DEPTH, D_MODEL, D_MODEL)) * D_MODEL ** -0.5 * res,
        'g_ffn': 1.0 + 0.02 * nrm(ks[15], (DEPTH, D_MODEL)),
        'w_ffn_in': nrm(ks[16], (DEPTH, D_MODEL, 2 * FFN_HIDDEN)) * D_MODEL ** -0.5,
        'w_ffn_out': nrm(ks[17], (DEPTH, FFN_HIDDEN, D_MODEL)) * FFN_HIDDEN ** -0.5 * res,
        'g_ple': 1.0 + 0.02 * nrm(ks[18], (DEPTH, D_MODEL)),
        'w_ple_gate': nrm(ks[19], (DEPTH, D_MODEL, D_MODEL)) * D_MODEL ** -0.5,
        'w_ple_proj': nrm(ks[20], (DEPTH, PLE_DIM, D_MODEL)) * PLE_DIM ** -0.5,
        'g_final': 1.0 + 0.02 * nrm(ks[21], (D_MODEL,)),
    }


def reference(x, p, g_mix, w_in, conv_a, sgu_ln_g, sgu_ln_b, sgu_w, sgu_b, conf_dw,
              conf_ln_g, conf_ln_b, w_branch, w_merge_gate, w_out, g_ffn, w_ffn_in,
              w_ffn_out, g_ple, w_ple_gate, w_ple_proj, g_final):
    for i in range(DEPTH):
        h = rmsnorm(x, g_mix[i])
        proj = h @ w_in[i]
        (a_b, a_c, a_x, q, k, v, s_u, s_v, c_val, c_gate) = jnp.split(proj, N_IN_BLOCKS, axis=-1)
        y_a = a_b * causal_dwconv(a_c * a_x, conv_a[i])
        y_b = dilated_attention(q, k, v)
        y_c = spatial_gating(s_u, s_v, sgu_ln_g[i], sgu_ln_b[i], sgu_w[i], sgu_b[i])
        y_d = conformer_conv(c_val, c_gate, conf_dw[i], conf_ln_g[i], conf_ln_b[i])
        branches = (y_a, y_b, y_c, y_d)
        merged = sum(jax.nn.sigmoid(h @ w_merge_gate[i, br]) * (branches[br] @ w_branch[i, br])
                     for br in range(N_BRANCH))
        x = x + merged @ w_out[i]
        h2 = rmsnorm(x, g_ffn[i])
        f_gate, f_up = jnp.split(h2 @ w_ffn_in[i], 2, axis=-1)
        x = x + (jax.nn.silu(f_gate) * f_up) @ w_ffn_out[i]
        h3 = rmsnorm(x, g_ple[i])
        x = x + jax.nn.sigmoid(h3 @ w_ple_gate[i]) * (p[i].astype(x.dtype) @ w_ple_proj[i])
    return rmsnorm(x, g_final)
```

```python
import functools

import jax
import jax.numpy as jnp
from jax import lax
from jax.experimental import pallas as pl
from jax.experimental.pallas import tpu as pltpu

F32 = jnp.float32
BF16 = jnp.bfloat16

EPS = 1e-6
N_BRANCH = 4
ATT_HEADS = 4
DSW_GROUPS = ((128, 1), (512, 4), (2048, 16))
ATT_BLK = 128
SGU_CHUNK = 128
SGU_GROUPS = 4
LANES = 128
HALO_A = 8
HALO_D = 32
CONV_ROWS = 64
FFN_CHUNK = 256
NEG = -1e30
VMEM_LIMIT = 56 * 1024 * 1024

TS_QKV = 1024
TS_MIX = 512
TS_FFN = 512
TQ_ATT = 512


def _rms(x, g):
    ms = jnp.mean(x * x, axis=-1, keepdims=True)
    return x * lax.rsqrt(ms + EPS) * g


def _ln(x, g, b):
    mu = jnp.mean(x, axis=-1, keepdims=True)
    xc = x - mu
    var = jnp.mean(xc * xc, axis=-1, keepdims=True)
    return xc * lax.rsqrt(var + EPS) * g + b


def _sigmoid(x):
    return 1.0 / (1.0 + jnp.exp(-x))


def _dot(a, b):
    return jnp.dot(a, b, preferred_element_type=F32)


def _const_spec(shape):
    nd = len(shape)
    return pl.BlockSpec(shape, lambda *_: (0,) * nd, pipeline_mode=pl.Buffered(1))


def _params(n_grid):
    return pltpu.CompilerParams(
        dimension_semantics=("arbitrary",) * n_grid,
        vmem_limit_bytes=VMEM_LIMIT)


def _qkv_body(x_ref, g_ref, w_ref, q_ref, k_ref, v_ref, *, bw, scale):
    h = _rms(x_ref[0], g_ref[...]).astype(BF16)
    qkv = _dot(h, w_ref[...])
    q_ref[0] = (qkv[:, 0:bw] * scale).astype(BF16)
    k_ref[0] = qkv[:, bw:2 * bw].astype(BF16)
    v_ref[0] = qkv[:, 2 * bw:3 * bw].astype(BF16)


def _qkv_call(x, g, w_qkv):
    b, s, d = x.shape
    bw = w_qkv.shape[1] // 3
    ts = min(TS_QKV, s)
    tile = lambda w: pl.BlockSpec((1, ts, w), lambda i, j: (i, j, 0))
    out = jax.ShapeDtypeStruct((b, s, bw), BF16)
    return pl.pallas_call(
        functools.partial(_qkv_body, bw=bw, scale=float((bw // ATT_HEADS) ** -0.5)),
        out_shape=(out, out, out),
        grid=(b, s // ts),
        in_specs=[tile(d), _const_spec((1, d)), _const_spec(w_qkv.shape)],
        out_specs=(tile(bw), tile(bw), tile(bw)),
        compiler_params=_params(2),
        name="qkv_proj",
    )(x, g, w_qkv)


def _attn_body(q_ref, kc_ref, kp_ref, vc_ref, vp_ref, hm_ref, o_ref, lse_ref,
               kbuf, vbuf, *, tq, bw):
    blk = ATT_BLK
    nh = ATT_HEADS
    dh = bw // nh
    j = pl.program_id(2)
    kbuf[0:blk, :] = kp_ref[0]
    kbuf[blk:blk + tq, :] = kc_ref[0]
    vbuf[0:blk, :] = vp_ref[0]
    vbuf[blk:blk + tq, :] = vc_ref[0]
    hm = hm_ref[...]
    qi = lax.broadcasted_iota(jnp.int32, (nh * blk, 2 * blk), 0) & (blk - 1)
    ki = lax.broadcasted_iota(jnp.int32, (nh * blk, 2 * blk), 1)
    band = (ki >= qi) & (ki <= qi + blk)
    first = jnp.where(j == 0, blk, 0)
    head_of_lane = lax.broadcasted_iota(jnp.int32, (blk, bw), 1) // dh
    for i in range(tq // blk):
        q = q_ref[0, i * blk:(i + 1) * blk, :]
        qst = jnp.concatenate([q] * nh, axis=0) * hm
        kk = kbuf[i * blk:(i + 2) * blk, :]
        vv = vbuf[i * blk:(i + 2) * blk, :]
        s = lax.dot_general(qst, kk, (((1,), (1,)), ((), ())),
                            preferred_element_type=F32)
        mask = band & (ki >= first) if i == 0 else band
        s = jnp.where(mask, s, NEG)
        m = jnp.max(s, axis=1, keepdims=True)
        p = jnp.exp(s - m)
        l = jnp.sum(p, axis=1, keepdims=True)
        o = _dot(p.astype(BF16), vv) * (1.0 / l)
        lse = m + jnp.log(l)
        oc = o[0:blk]
        lc = jnp.broadcast_to(lse[0:blk], (blk, bw))
        for h in range(1, nh):
            sel = head_of_lane == h
            oc = jnp.where(sel, o[h * blk:(h + 1) * blk], oc)
            lc = jnp.where(sel, lse[h * blk:(h + 1) * blk], lc)
        o_ref[0, i * blk:(i + 1) * blk, :] = oc
        lse_ref[0, i * blk:(i + 1) * blk, :] = lc


def _attn_call(q, k, v, hm, dil):
    b, s, bw = q.shape
    sd = s // dil
    tq = min(TQ_ATT, sd)
    nb = tq // ATT_BLK
    view = lambda t: t.reshape(b, sd, dil * bw)
    cur = pl.BlockSpec((1, tq, bw), lambda i, r, j: (i, j, r))
    prev = pl.BlockSpec((1, ATT_BLK, bw),
                        lambda i, r, j: (i, jnp.maximum(j * nb - 1, 0), r))
    out = jax.ShapeDtypeStruct((b, sd, dil * bw), F32)
    o, lse = pl.pallas_call(
        functools.partial(_attn_body, tq=tq, bw=bw),
        out_shape=(out, out),
        grid=(b, dil, sd // tq),
        in_specs=[cur, cur, prev, cur, prev, _const_spec(hm.shape)],
        out_specs=(cur, cur),
        scratch_shapes=[pltpu.VMEM((ATT_BLK + tq, bw), BF16),
                        pltpu.VMEM((ATT_BLK + tq, bw), BF16)],
        compiler_params=_params(3),
        name=f"band_attn_d{dil}",
    )(view(q), view(k), view(k), view(v), view(v), hm)
    return o.reshape(b, s, bw), lse.reshape(b, s, bw)


def _mix_body(x_ref, o1_ref, o2_ref, o3_ref, l1_ref, l2_ref, l3_ref,
              gmix_ref, win_ref, conva_ref, sgug_ref, sgub_ref, sguw_ref, sgubias_ref,
              confw_ref, confg_ref, confb_ref, wgate_ref, wbr_ref, wout_ref,
              xo_ref, exta, extd, convd, *, ts, bw, kd):
    nslab = bw // LANES

    @pl.when(pl.program_id(1) == 0)
    def _():
        exta[:, 0:HALO_A, :] = jnp.zeros((nslab, HALO_A, LANES), F32)
        extd[:, 0:HALO_D, :] = jnp.zeros((nslab, HALO_D, LANES), F32)

    x = x_ref[0]
    hb = _rms(x, gmix_ref[...]).astype(BF16)
    proj = _dot(hb, win_ref[...])
    a_b, a_c, a_x, s_u, s_v, c_val, c_gate = (
        proj[:, n * bw:(n + 1) * bw] for n in range(7))

    ca = a_c * a_x
    ka = conva_ref.shape[0]
    ya = []
    for sl in range(nslab):
        ls = slice(sl * LANES, (sl + 1) * LANES)
        exta[sl, HALO_A:HALO_A + ts, :] = ca[:, ls]
        acc = None
        for t in range(ka):
            off = HALO_A - (ka - 1) + t
            term = conva_ref[t:t + 1, ls] * exta[sl, off:off + ts, :]
            acc = term if acc is None else acc + term
        ya.append(a_b[:, ls] * acc)
        exta[sl, 0:HALO_A, :] = exta[sl, ts:ts + HALO_A, :]
    y_a = jnp.concatenate(ya, axis=1)

    l1, l2, l3 = l1_ref[0], l2_ref[0], l3_ref[0]
    lm = jnp.maximum(jnp.maximum(l1, l2), l3)
    e1, e2, e3 = jnp.exp(l1 - lm), jnp.exp(l2 - lm), jnp.exp(l3 - lm)
    y_b = (e1 * o1_ref[0] + e2 * o2_ref[0] + e3 * o3_ref[0]) / (e1 + e2 + e3)

    vb = _ln(s_v, sgug_ref[...], sgub_ref[...]).astype(BF16)
    ck = SGU_CHUNK
    wr = lax.broadcasted_iota(jnp.int32, (SGU_GROUPS * ck, ck), 0) & (ck - 1)
    wc = lax.broadcasted_iota(jnp.int32, (SGU_GROUPS * ck, ck), 1)
    wst = jnp.where(wc <= wr, sguw_ref[...], 0.0).astype(BF16)
    group_of_lane = lax.broadcasted_iota(jnp.int32, (ck, bw), 1) // (bw // SGU_GROUPS)
    yc = []
    for c in range(ts // ck):
        mm = _dot(wst, vb[c * ck:(c + 1) * ck, :])
        mixed = mm[0:ck]
        for g in range(1, SGU_GROUPS):
            mixed = jnp.where(group_of_lane == g, mm[g * ck:(g + 1) * ck], mixed)
        yc.append(s_u[c * ck:(c + 1) * ck, :] * (mixed + sgubias_ref[...]))
    y_c = jnp.concatenate(yc, axis=0)

    glu = c_val * _sigmoid(c_gate)
    for sl in range(nslab):
        extd[sl, HALO_D:HALO_D + ts, :] = glu[:, sl * LANES:(sl + 1) * LANES]

    def conv_step(r, carry):
        r0 = pl.multiple_of(r * CONV_ROWS, CONV_ROWS)
        for sl in range(nslab):
            ls = slice(sl * LANES, (sl + 1) * LANES)
            acc = None
            for t in range(kd):
                off = HALO_D - (kd - 1) + t
                term = confw_ref[t:t + 1, ls] * extd[sl, pl.ds(r0 + off, CONV_ROWS), :]
                acc = term if acc is None else acc + term
            convd[sl, pl.ds(r0, CONV_ROWS), :] = acc
        return carry

    lax.fori_loop(0, ts // CONV_ROWS, conv_step, 0)
    for sl in range(nslab):
        extd[sl, 0:HALO_D, :] = extd[sl, ts:ts + HALO_D, :]
    cd = jnp.concatenate([convd[sl] for sl in range(nslab)], axis=1)
    cd = _ln(cd, confg_ref[...], confb_ref[...])
    y_d = cd * _sigmoid(cd)

    merged = None
    for n, y in enumerate((y_a, y_b, y_c, y_d)):
        gate = _sigmoid(_dot(hb, wgate_ref[n]))
        term = gate * _dot(y.astype(BF16), wbr_ref[n])
        merged = term if merged is None else merged + term
    xo_ref[0] = x + _dot(merged.astype(BF16), wout_ref[...])


def _mix_call(x, o_g, lse_g, gmix, win, conva, sgug, sgub, sguw, sgubias,
              confw, confg, confb, wgate, wbr, wout):
    b, s, d = x.shape
    bw = conva.shape[1]
    kd = confw.shape[0]
    ts = min(TS_MIX, s)
    nslab = bw // LANES
    tile = lambda w: pl.BlockSpec((1, ts, w), lambda i, j: (i, j, 0))
    consts = (gmix, win, conva, sgug, sgub, sguw, sgubias, confw, confg, confb,
              wgate, wbr, wout)
    return pl.pallas_call(
        functools.partial(_mix_body, ts=ts, bw=bw, kd=kd),
        out_shape=jax.ShapeDtypeStruct((b, s, d), F32),
        grid=(b, s // ts),
        in_specs=[tile(d)] + [tile(bw)] * 6 + [_const_spec(c.shape) for c in consts],
        out_specs=tile(d),
        scratch_shapes=[pltpu.VMEM((nslab, HALO_A + ts, LANES), F32),
                        pltpu.VMEM((nslab, HALO_D + ts, LANES), F32),
                        pltpu.VMEM((nslab, ts, LANES), F32)],
        compiler_params=_params(2),
        name="mix_merge",
    )(x, *o_g, *lse_g, *consts)


def _ffn_body(x_ref, p_ref, gffn_ref, w1_ref, w2_ref, gple_ref, wpg_ref, wpp_ref,
              gfin_ref, xo_ref, hid_ref, *, fh, final):
    x = x_ref[0]
    hb = _rms(x, gffn_ref[...]).astype(BF16)
    for c in range(fh // FFN_CHUNK):
        c0 = c * FFN_CHUNK
        gate = _dot(hb, w1_ref[:, c0:c0 + FFN_CHUNK])
        up = _dot(hb, w1_ref[:, fh + c0:fh + c0 + FFN_CHUNK])
        hid_ref[:, c0:c0 + FFN_CHUNK] = (gate * _sigmoid(gate) * up).astype(BF16)
    x = x + _dot(hid_ref[...], w2_ref[...])
    h3 = _rms(x, gple_ref[...]).astype(BF16)
    x = x + _sigmoid(_dot(h3, wpg_ref[...])) * _dot(p_ref[0].astype(BF16), wpp_ref[...])
    if final:
        x = _rms(x, gfin_ref[...])
    xo_ref[0] = x


def _ffn_call(x, p, gffn, w1, w2, gple, wpg, wpp, gfin, final):
    b, s, d = x.shape
    fh = w2.shape[0]
    ts = min(TS_FFN, s)
    tile = lambda w: pl.BlockSpec((1, ts, w), lambda i, j: (i, j, 0))
    consts = (gffn, w1, w2, gple, wpg, wpp, gfin)
    return pl.pallas_call(
        functools.partial(_ffn_body, fh=fh, final=final),
        out_shape=jax.ShapeDtypeStruct((b, s, d), F32),
        grid=(b, s // ts),
        in_specs=[tile(d), tile(p.shape[-1])] + [_const_spec(c.shape) for c in consts],
        out_specs=tile(d),
        scratch_shapes=[pltpu.VMEM((ts, fh), BF16)],
        compiler_params=_params(2),
        name="ffn_ple",
    )(x, p, *consts)


def kernel(x, p, g_mix, w_in, conv_a, sgu_ln_g, sgu_ln_b, sgu_w, sgu_b, conf_dw,
           conf_ln_g, conf_ln_b, w_branch, w_merge_gate, w_out, g_ffn, w_ffn_in,
           w_ffn_out, g_ple, w_ple_gate, w_ple_proj, g_final):
    depth = w_in.shape[0]
    d = x.shape[-1]
    bw = d // N_BRANCH
    assert all(w // dil == ATT_BLK for w, dil in DSW_GROUPS)
    row = lambda a: a.reshape(1, -1)
    head_of_col = jnp.arange(bw, dtype=jnp.int32) // (bw // ATT_HEADS)
    head_of_row = jnp.arange(ATT_HEADS * ATT_BLK, dtype=jnp.int32) // ATT_BLK
    hm = (head_of_row[:, None] == head_of_col[None, :]).astype(BF16)
    for i in range(depth):
        wi = w_in[i].astype(BF16)
        q, k, v = _qkv_call(x, row(g_mix[i]), wi[:, 3 * bw:6 * bw])
        groups = [_attn_call(q, k, v, hm, dil) for _, dil in DSW_GROUPS]
        win_rest = jnp.concatenate([wi[:, :3 * bw], wi[:, 6 * bw:]], axis=1)
        sgubias = jnp.repeat(sgu_b[i].T, bw // SGU_GROUPS, axis=1)
        x = _mix_call(
            x, [o for o, _ in groups], [l for _, l in groups], row(g_mix[i]), win_rest,
            conv_a[i], row(sgu_ln_g[i]), row(sgu_ln_b[i]),
            sgu_w[i].reshape(SGU_GROUPS * SGU_CHUNK, SGU_CHUNK), sgubias,
            conf_dw[i], row(conf_ln_g[i]), row(conf_ln_b[i]),
            w_merge_gate[i].astype(BF16), w_branch[i].astype(BF16), w_out[i].astype(BF16))
        x = _ffn_call(
            x, p[i], row(g_ffn[i]), w_ffn_in[i].astype(BF16), w_ffn_out[i].astype(BF16),
            row(g_ple[i]), w_ple_gate[i].astype(BF16), w_ple_proj[i].astype(BF16),
            row(g_final), final=(i == depth - 1))
    return x
```

```python
import functools

import jax
import jax.numpy as jnp
from jax import lax
from jax.experimental import pallas as pl
from jax.experimental.pallas import tpu as pltpu

F32 = jnp.float32
BF16 = jnp.bfloat16

EPS = 1e-6
N_BRANCH = 4
ATT_HEADS = 4
DSW_GROUPS = ((128, 1), (512, 4), (2048, 16))
ATT_BLK = 128
SGU_CHUNK = 128
SGU_GROUPS = 4
LANES = 128
HALO_A = 8
HALO_D = 32
CONV_ROWS = 64
MERGE_ROWS = 256
FFN_CHUNK = 256
NEG = -1e30
VMEM_LIMIT = 56 * 1024 * 1024

TS_QKV = 1024
TS_MIX = 512
TS_FFN = 512


def _rms(x, g):
    ms = jnp.mean(x * x, axis=-1, keepdims=True)
    return x * lax.rsqrt(ms + EPS) * g


def _ln(x, g, b):
    mu = jnp.mean(x, axis=-1, keepdims=True)
    xc = x - mu
    var = jnp.mean(xc * xc, axis=-1, keepdims=True)
    return xc * lax.rsqrt(var + EPS) * g + b


def _sigmoid(x):
    return 1.0 / (1.0 + jnp.exp(-x))


def _dot(a, b):
    return jnp.dot(a, b, preferred_element_type=F32)


def _const_spec(shape):
    nd = len(shape)
    return pl.BlockSpec(shape, lambda *_: (0,) * nd, pipeline_mode=pl.Buffered(1))


def _params(n_grid):
    return pltpu.CompilerParams(
        dimension_semantics=("arbitrary",) * n_grid,
        vmem_limit_bytes=VMEM_LIMIT)


def _qkv_body(x_ref, g_ref, w_ref, *rest, ts, bw, scale, dils):
    outs, stage = rest[:-1], rest[-1]
    h = _rms(x_ref[0], g_ref[...]).astype(BF16)
    qkv = _dot(h, w_ref[...])
    nsl = bw // LANES
    for c in range(3 * nsl):
        col = qkv[:, c * LANES:(c + 1) * LANES]
        stage[c] = col * scale if c < nsl else col
    for n, dil in enumerate(dils):
        for t in range(3):
            ref = outs[3 * n + t]
            for r in range(dil):
                for sl in range(nsl):
                    if dil == 1:
                        rows = stage[t * nsl + sl]
                    else:
                        rows = stage[t * nsl + sl, pl.ds(r, ts // dil, stride=dil), :]
                    lo = r * bw + sl * LANES
                    ref[0, :, lo:lo + LANES] = rows.astype(BF16)


def _qkv_call(x, g, w_qkv, dils):
    b, s, d = x.shape
    bw = w_qkv.shape[1] // 3
    ts = min(TS_QKV, s)
    out_shape, out_specs = [], []
    for dil in dils:
        for _ in range(3):
            out_shape.append(jax.ShapeDtypeStruct((b, s // dil, dil * bw), BF16))
            out_specs.append(pl.BlockSpec((1, ts // dil, dil * bw), lambda i, j: (i, j, 0)))
    return pl.pallas_call(
        functools.partial(_qkv_body, ts=ts, bw=bw, dils=dils,
                          scale=float((bw // ATT_HEADS) ** -0.5)),
        out_shape=tuple(out_shape),
        grid=(b, s // ts),
        in_specs=[pl.BlockSpec((1, ts, d), lambda i, j: (i, j, 0)),
                  _const_spec((1, d)), _const_spec(w_qkv.shape)],
        out_specs=tuple(out_specs),
        scratch_shapes=[pltpu.VMEM((3 * bw // LANES, ts, LANES), F32)],
        compiler_params=_params(2),
        name="qkv_proj",
    )(x, g, w_qkv)


def _attn_block(q, kk, vv, hm, first, bw):
    blk = ATT_BLK
    nh = ATT_HEADS
    qi = lax.broadcasted_iota(jnp.int32, (nh * blk, 2 * blk), 0) & (blk - 1)
    ki = lax.broadcasted_iota(jnp.int32, (nh * blk, 2 * blk), 1)
    lo = jnp.maximum(qi, jnp.where(first, blk, 0))
    mask = (ki >= lo) & (ki <= qi + blk)
    qst = jnp.concatenate([q] * nh, axis=0) * hm
    s = lax.dot_general(qst, kk, (((1,), (1,)), ((), ())), preferred_element_type=F32)
    s = jnp.where(mask, s, NEG)
    m = jnp.max(s, axis=1, keepdims=True)
    p = jnp.exp(s - m)
    l = jnp.sum(p, axis=1, keepdims=True)
    o = _dot(p.astype(BF16), vv) * (1.0 / l)
    lse = m + jnp.log(l)
    head_of_lane = lax.broadcasted_iota(jnp.int32, (blk, bw), 1) // (bw // nh)
    oc = o[0:blk]
    lc = jnp.broadcast_to(lse[0:blk], (blk, bw))
    for h in range(1, nh):
        sel = head_of_lane == h
        oc = jnp.where(sel, o[h * blk:(h + 1) * blk], oc)
        lc = jnp.where(sel, lse[h * blk:(h + 1) * blk], lc)
    return oc, lc


def _attn_body(*refs, bw, dils, span):
    ng = len(dils)
    hm_ref, yb_ref, onat, lnat = refs[5 * ng:]
    blk = ATT_BLK
    nsl = bw // LANES
    j = pl.program_id(1)
    u = pl.program_id(2)
    hm = hm_ref[...]
    for g, dil in enumerate(dils):
        q_ref, kc_ref, kp_ref, vc_ref, vp_ref = refs[5 * g:5 * g + 5]
        per_row = span // (blk * dil)
        sub = u // dil if per_row > 1 else 0
        res = u % dil if dil > 1 else 0
        first = (j == 0) & (sub == 0)
        kk = jnp.concatenate([kp_ref[0], kc_ref[0]], axis=0)
        vv = jnp.concatenate([vp_ref[0], vc_ref[0]], axis=0)
        oc, lc = _attn_block(q_ref[0], kk, vv, hm, first, bw)
        start = sub * (blk * dil) + res
        for sl in range(nsl):
            ls = slice(sl * LANES, (sl + 1) * LANES)
            if dil == 1:
                idx = pl.ds(pl.multiple_of(start, blk), blk)
            else:
                idx = pl.ds(start, blk, stride=dil)
            onat[g, sl, idx, :] = oc[:, ls]
            lnat[g, sl, idx, :] = lc[:, ls]

    @pl.when(u == pl.num_programs(2) - 1)
    def _():
        def merge(c, carry):
            rows = pl.ds(pl.multiple_of(c * MERGE_ROWS, MERGE_ROWS), MERGE_ROWS)
            for sl in range(nsl):
                ls = [lnat[g, sl, rows, :] for g in range(ng)]
                lm = functools.reduce(jnp.maximum, ls)
                es = [jnp.exp(l - lm) for l in ls]
                num = sum(e * onat[g, sl, rows, :] for g, e in enumerate(es))
                yb_ref[0, rows, sl * LANES:(sl + 1) * LANES] = (num / sum(es)).astype(BF16)
            return carry
        lax.fori_loop(0, span // MERGE_ROWS, merge, 0)


def _attn_call(qkv, hm, dils, bw):
    b, s, _ = qkv[0].shape
    blk = ATT_BLK
    span = blk * max(dils)
    nslot = span // blk
    in_specs, args = [], []
    for g, dil in enumerate(dils):
        per_row = span // (blk * dil)
        if dil == 1:
            cur = lambda i, j, u: (i, j * nslot + u, 0)
            prev = lambda i, j, u: (i, jnp.maximum(j * nslot + u - 1, 0), 0)
        elif per_row > 1:
            cur = lambda i, j, u, d=dil, n=per_row: (i, j * n + u // d, u % d)
            prev = lambda i, j, u, d=dil, n=per_row: (
                i, jnp.maximum(j * n + u // d - 1, 0), u % d)
        else:
            cur = lambda i, j, u: (i, j, u)
            prev = lambda i, j, u: (i, jnp.maximum(j - 1, 0), u)
        q, k, v = qkv[3 * g:3 * g + 3]
        bs = (1, blk, bw)
        in_specs += [pl.BlockSpec(bs, cur), pl.BlockSpec(bs, cur), pl.BlockSpec(bs, prev),
                     pl.BlockSpec(bs, cur), pl.BlockSpec(bs, prev)]
        args += [q, k, k, v, v]
    nsl = bw // LANES
    return pl.pallas_call(
        functools.partial(_attn_body, bw=bw, dils=dils, span=span),
        out_shape=jax.ShapeDtypeStruct((b, s, bw), BF16),
        grid=(b, s // span, nslot),
        in_specs=in_specs + [_const_spec(hm.shape)],
        out_specs=pl.BlockSpec((1, span, bw), lambda i, j, u: (i, j, 0)),
        scratch_shapes=[pltpu.VMEM((len(dils), nsl, span, LANES), F32),
                        pltpu.VMEM((len(dils), nsl, span, LANES), F32)],
        compiler_params=_params(3),
        name="band_attn",
    )(*args, hm)


def _mix_body(x_ref, yb_ref,
              gmix_ref, win_ref, conva_ref, sgug_ref, sgub_ref, sguw_ref, sgubias_ref,
              confw_ref, confg_ref, confb_ref, wgate_ref, wbr_ref, wout_ref,
              xo_ref, exta, extd, *, ts, bw, kd):
    nslab = bw // LANES

    @pl.when(pl.program_id(1) == 0)
    def _():
        exta[:, 0:HALO_A, :] = jnp.zeros((nslab, HALO_A, LANES), F32)
        extd[:, 0:HALO_D, :] = jnp.zeros((nslab, HALO_D, LANES), F32)

    x = x_ref[0]
    hb = _rms(x, gmix_ref[...]).astype(BF16)
    proj = _dot(hb, win_ref[...])
    a_b, a_c, a_x, s_u, s_v, c_val, c_gate = (
        proj[:, n * bw:(n + 1) * bw] for n in range(7))

    ca = a_c * a_x
    ka = conva_ref.shape[0]
    ya = []
    for sl in range(nslab):
        ls = slice(sl * LANES, (sl + 1) * LANES)
        exta[sl, HALO_A:HALO_A + ts, :] = ca[:, ls]
        acc = None
        for t in range(ka):
            off = HALO_A - (ka - 1) + t
            term = conva_ref[t:t + 1, ls] * exta[sl, off:off + ts, :]
            acc = term if acc is None else acc + term
        ya.append(a_b[:, ls] * acc)
        exta[sl, 0:HALO_A, :] = exta[sl, ts:ts + HALO_A, :]
    y_a = jnp.concatenate(ya, axis=1)

    vb = _ln(s_v, sgug_ref[...], sgub_ref[...]).astype(BF16)
    ck = SGU_CHUNK
    wr = lax.broadcasted_iota(jnp.int32, (SGU_GROUPS * ck, ck), 0) & (ck - 1)
    wc = lax.broadcasted_iota(jnp.int32, (SGU_GROUPS * ck, ck), 1)
    wst = jnp.where(wc <= wr, sguw_ref[...], 0.0).astype(BF16)
    group_of_lane = lax.broadcasted_iota(jnp.int32, (ck, bw), 1) // (bw // SGU_GROUPS)
    yc = []
    for c in range(ts // ck):
        mm = _dot(wst, vb[c * ck:(c + 1) * ck, :])
        mixed = mm[0:ck]
        for g in range(1, SGU_GROUPS):
            mixed = jnp.where(group_of_lane == g, mm[g * ck:(g + 1) * ck], mixed)
        yc.append(s_u[c * ck:(c + 1) * ck, :] * (mixed + sgubias_ref[...]))
    y_c = jnp.concatenate(yc, axis=0)

    glu = c_val * _sigmoid(c_gate)
    for sl in range(nslab):
        extd[sl, HALO_D:HALO_D + ts, :] = glu[:, sl * LANES:(sl + 1) * LANES]
    cd = []
    for sl in range(nslab):
        ls = slice(sl * LANES, (sl + 1) * LANES)
        pieces = []
        for r0 in range(0, ts, CONV_ROWS):
            acc = None
            for t in range(kd):
                off = r0 + HALO_D - (kd - 1) + t
                term = confw_ref[t:t + 1, ls] * extd[sl, off:off + CONV_ROWS, :]
                acc = term if acc is None else acc + term
            pieces.append(acc)
        cd.append(jnp.concatenate(pieces, axis=0))
        extd[sl, 0:HALO_D, :] = extd[sl, ts:ts + HALO_D, :]
    cd = _ln(jnp.concatenate(cd, axis=1), confg_ref[...], confb_ref[...])
    y_d = cd * _sigmoid(cd)

    merged = None
    for n, y in enumerate((y_a, yb_ref[0], y_c, y_d)):
        gate = _sigmoid(_dot(hb, wgate_ref[n]))
        term = gate * _dot(y.astype(BF16), wbr_ref[n])
        merged = term if merged is None else merged + term
    xo_ref[0] = x + _dot(merged.astype(BF16), wout_ref[...])


def _mix_call(x, y_b, gmix, win, conva, sgug, sgub, sguw, sgubias,
              confw, confg, confb, wgate, wbr, wout):
    b, s, d = x.shape
    bw = conva.shape[1]
    kd = confw.shape[0]
    ts = min(TS_MIX, s)
    nslab = bw // LANES
    tile = lambda w: pl.BlockSpec((1, ts, w), lambda i, j: (i, j, 0))
    consts = (gmix, win, conva, sgug, sgub, sguw, sgubias, confw, confg, confb,
              wgate, wbr, wout)
    return pl.pallas_call(
        functools.partial(_mix_body, ts=ts, bw=bw, kd=kd),
        out_shape=jax.ShapeDtypeStruct((b, s, d), F32),
        grid=(b, s // ts),
        in_specs=[tile(d), tile(bw)] + [_const_spec(c.shape) for c in consts],
        out_specs=tile(d),
        scratch_shapes=[pltpu.VMEM((nslab, HALO_A + ts, LANES), F32),
                        pltpu.VMEM((nslab, HALO_D + ts, LANES), F32)],
        compiler_params=_params(2),
        name="mix_merge",
    )(x, y_b, *consts)


def _ffn_body(x_ref, p_ref, gffn_ref, w1_ref, w2_ref, gple_ref, wpg_ref, wpp_ref,
              gfin_ref, xo_ref, hid_ref, *, fh, final):
    x = x_ref[0]
    hb = _rms(x, gffn_ref[...]).astype(BF16)
    for c in range(fh // FFN_CHUNK):
        c0 = c * FFN_CHUNK
        gate = _dot(hb, w1_ref[:, c0:c0 + FFN_CHUNK])
        up = _dot(hb, w1_ref[:, fh + c0:fh + c0 + FFN_CHUNK])
        hid_ref[:, c0:c0 + FFN_CHUNK] = (gate * _sigmoid(gate) * up).astype(BF16)
    x = x + _dot(hid_ref[...], w2_ref[...])
    h3 = _rms(x, gple_ref[...]).astype(BF16)
    x = x + _sigmoid(_dot(h3, wpg_ref[...])) * _dot(p_ref[0].astype(BF16), wpp_ref[...])
    if final:
        x = _rms(x, gfin_ref[...])
    xo_ref[0] = x


def _ffn_call(x, p, gffn, w1, w2, gple, wpg, wpp, gfin, final):
    b, s, d = x.shape
    fh = w2.shape[0]
    ts = min(TS_FFN, s)
    tile = lambda w: pl.BlockSpec((1, ts, w), lambda i, j: (i, j, 0))
    consts = (gffn, w1, w2, gple, wpg, wpp, gfin)
    return pl.pallas_call(
        functools.partial(_ffn_body, fh=fh, final=final),
        out_shape=jax.ShapeDtypeStruct((b, s, d), F32),
        grid=(b, s // ts),
        in_specs=[tile(d), tile(p.shape[-1])] + [_const_spec(c.shape) for c in consts],
        out_specs=tile(d),
        scratch_shapes=[pltpu.VMEM((ts, fh), BF16)],
        compiler_params=_params(2),
        name="ffn_ple",
    )(x, p, *consts)


def kernel(x, p, g_mix, w_in, conv_a, sgu_ln_g, sgu_ln_b, sgu_w, sgu_b, conf_dw,
           conf_ln_g, conf_ln_b, w_branch, w_merge_gate, w_out, g_ffn, w_ffn_in,
           w_ffn_out, g_ple, w_ple_gate, w_ple_proj, g_final):
    depth = w_in.shape[0]
    d = x.shape[-1]
    bw = d // N_BRANCH
    assert all(w // dil == ATT_BLK for w, dil in DSW_GROUPS)
    dils = tuple(dil for _, dil in DSW_GROUPS)
    row = lambda a: a.reshape(1, -1)
    head_of_col = jnp.arange(bw, dtype=jnp.int32) // (bw // ATT_HEADS)
    head_of_row = jnp.arange(ATT_HEADS * ATT_BLK, dtype=jnp.int32) // ATT_BLK
    hm = (head_of_row[:, None] == head_of_col[None, :]).astype(BF16)
    for i in range(depth):
        wi = w_in[i].astype(BF16)
        qkv = _qkv_call(x, row(g_mix[i]), wi[:, 3 * bw:6 * bw], dils)
        y_b = _attn_call(qkv, hm, dils, bw)
        win_rest = jnp.concatenate([wi[:, :3 * bw], wi[:, 6 * bw:]], axis=1)
        sgubias = jnp.repeat(sgu_b[i].T, bw // SGU_GROUPS, axis=1)
        x = _mix_call(
            x, y_b, row(g_mix[i]), win_rest,
            conv_a[i], row(sgu_ln_g[i]), row(sgu_ln_b[i]),
            sgu_w[i].reshape(SGU_GROUPS * SGU_CHUNK, SGU_CHUNK), sgubias,
            conf_dw[i], row(conf_ln_g[i]), row(conf_ln_b[i]),
            w_merge_gate[i].astype(BF16), w_branch[i].astype(BF16), w_out[i].astype(BF16))
        x = _ffn_call(
            x, p[i], row(g_ffn[i]), w_ffn_in[i].astype(BF16), w_ffn_out[i].astype(BF16),
            row(g_ple[i]), w_ple_gate[i].astype(BF16), w_ple_proj[i].astype(BF16),
            row(g_final), final=(i == depth - 1))
    return x
```

```python
import functools

import jax
import jax.numpy as jnp
from jax import lax
from jax.experimental import pallas as pl
from jax.experimental.pallas import tpu as pltpu

F32 = jnp.float32
BF16 = jnp.bfloat16

EPS = 1e-6
N_BRANCH = 4
ATT_HEADS = 4
DSW_GROUPS = ((128, 1), (512, 4), (2048, 16))
ATT_BLK = 128
ATT_SLOTS = 4
SGU_CHUNK = 128
SGU_GROUPS = 4
LANES = 128
HALO_A = 8
HALO_D = 32
CONV_ROWS = 64
MERGE_ROWS = 256
FFN_CHUNK = 256
NEG = -1e30
LOG2E = 1.4426950408889634
VMEM_LIMIT = 56 * 1024 * 1024

TS_QKV = 1024
TS_MIX = 512
TS_FFN = 512


def _rms(x, g):
    ms = jnp.mean(x * x, axis=-1, keepdims=True)
    return x * lax.rsqrt(ms + EPS) * g


def _ln(x, g, b):
    mu = jnp.mean(x, axis=-1, keepdims=True)
    xc = x - mu
    var = jnp.mean(xc * xc, axis=-1, keepdims=True)
    return xc * lax.rsqrt(var + EPS) * g + b


def _sigmoid(x):
    return 1.0 / (1.0 + jnp.exp(-x))


def _dot(a, b):
    return jnp.dot(a, b, preferred_element_type=F32)


def _const_spec(shape):
    nd = len(shape)
    return pl.BlockSpec(shape, lambda *_: (0,) * nd, pipeline_mode=pl.Buffered(1))


def _params(n_grid):
    return pltpu.CompilerParams(
        dimension_semantics=("arbitrary",) * n_grid,
        vmem_limit_bytes=VMEM_LIMIT)


def _emit_qkv(x, g_ref, w_ref, outs, stage, *, ts, bw, dils):
    h = _rms(x, g_ref[...]).astype(BF16)
    qkv = _dot(h, w_ref[...])
    nsl = bw // LANES
    scale = float((bw // ATT_HEADS) ** -0.5 * LOG2E)
    for c in range(3 * nsl):
        col = qkv[:, c * LANES:(c + 1) * LANES]
        stage[c] = col * scale if c < nsl else col
    for n, dil in enumerate(dils):
        for t in range(3):
            ref = outs[3 * n + t]
            for r in range(dil):
                for sl in range(nsl):
                    if dil == 1:
                        rows = stage[t * nsl + sl]
                    else:
                        rows = stage[t * nsl + sl, pl.ds(r, ts // dil, stride=dil), :]
                    lo = r * bw + sl * LANES
                    ref[0, :, lo:lo + LANES] = rows.astype(BF16)


def _qkv_outs(b, s, ts, bw, dils):
    out_shape, out_specs = [], []
    for dil in dils:
        for _ in range(3):
            out_shape.append(jax.ShapeDtypeStruct((b, s // dil, dil * bw), BF16))
            out_specs.append(pl.BlockSpec((1, ts // dil, dil * bw), lambda i, j: (i, j, 0)))
    return out_shape, out_specs


def _qkv_body(x_ref, g_ref, w_ref, *rest, ts, bw, dils):
    _emit_qkv(x_ref[0], g_ref, w_ref, rest[:-1], rest[-1], ts=ts, bw=bw, dils=dils)


def _qkv_call(x, g, w_qkv, dils):
    b, s, d = x.shape
    bw = w_qkv.shape[1] // 3
    ts = min(TS_QKV, s)
    out_shape, out_specs = _qkv_outs(b, s, ts, bw, dils)
    return pl.pallas_call(
        functools.partial(_qkv_body, ts=ts, bw=bw, dils=dils),
        out_shape=tuple(out_shape),
        grid=(b, s // ts),
        in_specs=[pl.BlockSpec((1, ts, d), lambda i, j: (i, j, 0)),
                  _const_spec((1, d)), _const_spec(w_qkv.shape)],
        out_specs=tuple(out_specs),
        scratch_shapes=[pltpu.VMEM((3 * bw // LANES, ts, LANES), F32)],
        compiler_params=_params(2),
        name="qkv_proj",
    )(x, g, w_qkv)


def _attn_block(q, kk, vv, hm, bias, first, bw):
    blk = ATT_BLK
    nh = ATT_HEADS
    dh = bw // nh
    hps = LANES // dh
    qst = jnp.concatenate([q] * nh, axis=0) * hm
    s = lax.dot_general(qst, kk, (((1,), (1,)), ((), ())), preferred_element_type=F32)
    s = s + bias
    if first is not False:
        s = jnp.concatenate([jnp.where(first, NEG, s[:, :blk]), s[:, blk:]], axis=1)
    m = jnp.max(s, axis=1, keepdims=True)
    p = jnp.exp2(s - m).astype(BF16)
    ones = jnp.ones((2 * blk, LANES), BF16)
    head_in_slab = lax.broadcasted_iota(jnp.int32, (blk, LANES), 1) // dh
    out = []
    for sl in range(bw // LANES):
        r0 = sl * hps * blk
        rhs = jnp.concatenate([vv[:, sl * LANES:(sl + 1) * LANES], ones], axis=1)
        ol = _dot(p[r0:r0 + hps * blk], rhs)
        o_s, l_s = ol[0:blk, :LANES], ol[0:blk, LANES:]
        m_s = jnp.broadcast_to(m[r0:r0 + blk], (blk, LANES))
        for hh in range(1, hps):
            sel = head_in_slab == hh
            rows = slice(hh * blk, (hh + 1) * blk)
            o_s = jnp.where(sel, ol[rows, :LANES], o_s)
            l_s = jnp.where(sel, ol[rows, LANES:], l_s)
            m_s = jnp.where(sel, m[r0 + hh * blk:r0 + (hh + 1) * blk], m_s)
        out.append((o_s, m_s, l_s))
    return out


def _attn_tiling(dil, span):
    rows = max(ATT_SLOTS // dil, 1)
    cols = min(dil, ATT_SLOTS)
    row_steps = span // (ATT_BLK * dil) // rows
    col_steps = dil // cols
    return rows, cols, row_steps, col_steps


def _attn_body(*refs, bw, dils, span):
    ng = len(dils)
    hm_ref, bias_ref, yb_ref, onat, mnat, lnat = refs[5 * ng:]
    blk = ATT_BLK
    nsl = bw // LANES
    j = pl.program_id(1)
    it = pl.program_id(2)
    hm = hm_ref[...]
    bias = bias_ref[...]
    for g, dil in enumerate(dils):
        q_ref, kc_ref, kp_ref, vc_ref, vp_ref = refs[5 * g:5 * g + 5]
        rows, cols, _, col_steps = _attn_tiling(dil, span)
        row_step = it // col_steps if col_steps > 1 else it
        col_step = it % col_steps if col_steps > 1 else 0
        for a in range(rows):
            for c in range(cols):
                cs = slice(c * bw, (c + 1) * bw)
                q = q_ref[0, a * blk:(a + 1) * blk, cs]
                if a == 0:
                    kk = jnp.concatenate([kp_ref[0, :, cs], kc_ref[0, 0:blk, cs]], axis=0)
                    vv = jnp.concatenate([vp_ref[0, :, cs], vc_ref[0, 0:blk, cs]], axis=0)
                    first = (j == 0) & (row_step == 0)
                else:
                    kk = kc_ref[0, (a - 1) * blk:(a + 1) * blk, cs]
                    vv = vc_ref[0, (a - 1) * blk:(a + 1) * blk, cs]
                    first = False
                stats = _attn_block(q, kk, vv, hm, bias, first, bw)
                start = (row_step * rows + a) * (blk * dil) + col_step * cols + c
                if dil == 1:
                    idx = pl.ds(pl.multiple_of(start, blk), blk)
                else:
                    idx = pl.ds(start, blk, stride=dil)
                for sl, (o_s, m_s, l_s) in enumerate(stats):
                    onat[g, sl, idx, :] = o_s
                    mnat[g, sl, idx, :] = m_s
                    lnat[g, sl, idx, :] = l_s

    @pl.when(it == pl.num_programs(2) - 1)
    def _():
        def merge(c, carry):
            rows = pl.ds(pl.multiple_of(c * MERGE_ROWS, MERGE_ROWS), MERGE_ROWS)
            for sl in range(nsl):
                ms = [mnat[g, sl, rows, :] for g in range(ng)]
                mm = functools.reduce(jnp.maximum, ms)
                ws = [jnp.exp2(m - mm) for m in ms]
                num = sum(w * onat[g, sl, rows, :] for g, w in enumerate(ws))
                den = sum(w * lnat[g, sl, rows, :] for g, w in enumerate(ws))
                yb_ref[0, rows, sl * LANES:(sl + 1) * LANES] = (num / den).astype(BF16)
            return carry
        lax.fori_loop(0, span // MERGE_ROWS, merge, 0)


def _attn_call(qkv, hm, bias, dils, bw):
    b, s, _ = qkv[0].shape
    blk = ATT_BLK
    span = blk * max(dils)
    nsteps = span // blk // ATT_SLOTS
    in_specs, args = [], []
    for g, dil in enumerate(dils):
        rows, cols, row_steps, col_steps = _attn_tiling(dil, span)
        assert row_steps * col_steps == nsteps

        def cur(i, j, it, rs=row_steps, cst=col_steps):
            return (i, j * rs + it // cst, it % cst)

        def prev(i, j, it, rs=row_steps, cst=col_steps, r=rows):
            return (i, jnp.maximum((j * rs + it // cst) * r - 1, 0), it % cst)

        q, k, v = qkv[3 * g:3 * g + 3]
        cur_spec = pl.BlockSpec((1, rows * blk, cols * bw), cur)
        prev_spec = pl.BlockSpec((1, blk, cols * bw), prev)
        in_specs += [cur_spec, cur_spec, prev_spec, cur_spec, prev_spec]
        args += [q, k, k, v, v]
    nsl = bw // LANES
    return pl.pallas_call(
        functools.partial(_attn_body, bw=bw, dils=dils, span=span),
        out_shape=jax.ShapeDtypeStruct((b, s, bw), BF16),
        grid=(b, s // span, nsteps),
        in_specs=in_specs + [_const_spec(hm.shape), _const_spec(bias.shape)],
        out_specs=pl.BlockSpec((1, span, bw), lambda i, j, u: (i, j, 0)),
        scratch_shapes=[pltpu.VMEM((len(dils), nsl, span, LANES), F32)] * 3,
        compiler_params=_params(3),
        name="band_attn",
    )(*args, hm, bias)


def _mix_body(x_ref, yb_ref,
              gmix_ref, win_ref, conva_ref, sgug_ref, sgub_ref, sguw_ref, sgubias_ref,
              confw_ref, confg_ref, confb_ref, wgate_ref, wbr_ref, wout_ref,
              xo_ref, exta, extd, *, ts, bw, kd):
    nslab = bw // LANES

    @pl.when(pl.program_id(1) == 0)
    def _():
        exta[:, 0:HALO_A, :] = jnp.zeros((nslab, HALO_A, LANES), F32)
        extd[:, 0:HALO_D, :] = jnp.zeros((nslab, HALO_D, LANES), F32)

    x = x_ref[0]
    hb = _rms(x, gmix_ref[...]).astype(BF16)
    proj = _dot(hb, win_ref[...])
    a_b, a_c, a_x, s_u, s_v, c_val, c_gate = (
        proj[:, n * bw:(n + 1) * bw] for n in range(7))

    ca = a_c * a_x
    ka = conva_ref.shape[0]
    ya = []
    for sl in range(nslab):
        ls = slice(sl * LANES, (sl + 1) * LANES)
        exta[sl, HALO_A:HALO_A + ts, :] = ca[:, ls]
        acc = None
        for t in range(ka):
            off = HALO_A - (ka - 1) + t
            term = conva_ref[t:t + 1, ls] * exta[sl, off:off + ts, :]
            acc = term if acc is None else acc + term
        ya.append(a_b[:, ls] * acc)
        exta[sl, 0:HALO_A, :] = exta[sl, ts:ts + HALO_A, :]
    y_a = jnp.concatenate(ya, axis=1)

    vb = _ln(s_v, sgug_ref[...], sgub_ref[...]).astype(BF16)
    ck = SGU_CHUNK
    wr = lax.broadcasted_iota(jnp.int32, (SGU_GROUPS * ck, ck), 0) & (ck - 1)
    wc = lax.broadcasted_iota(jnp.int32, (SGU_GROUPS * ck, ck), 1)
    wst = jnp.where(wc <= wr, sguw_ref[...], 0.0).astype(BF16)
    group_of_lane = lax.broadcasted_iota(jnp.int32, (ck, bw), 1) // (bw // SGU_GROUPS)
    yc = []
    for c in range(ts // ck):
        mm = _dot(wst, vb[c * ck:(c + 1) * ck, :])
        mixed = mm[0:ck]
        for g in range(1, SGU_GROUPS):
            mixed = jnp.where(group_of_lane == g, mm[g * ck:(g + 1) * ck], mixed)
        yc.append(s_u[c * ck:(c + 1) * ck, :] * (mixed + sgubias_ref[...]))
    y_c = jnp.concatenate(yc, axis=0)

    glu = c_val * _sigmoid(c_gate)
    for sl in range(nslab):
        extd[sl, HALO_D:HALO_D + ts, :] = glu[:, sl * LANES:(sl + 1) * LANES]
    cd = []
    for sl in range(nslab):
        ls = slice(sl * LANES, (sl + 1) * LANES)
        pieces = []
        for r0 in range(0, ts, CONV_ROWS):
            acc = None
            for t in range(kd):
                off = r0 + HALO_D - (kd - 1) + t
                term = confw_ref[t:t + 1, ls] * extd[sl, off:off + CONV_ROWS, :]
                acc = term if acc is None else acc + term
            pieces.append(acc)
        cd.append(jnp.concatenate(pieces, axis=0))
        extd[sl, 0:HALO_D, :] = extd[sl, ts:ts + HALO_D, :]
    cd = _ln(jnp.concatenate(cd, axis=1), confg_ref[...], confb_ref[...])
    y_d = cd * _sigmoid(cd)

    merged = None
    for n, y in enumerate((y_a, yb_ref[0], y_c, y_d)):
        gate = _sigmoid(_dot(hb, wgate_ref[n]))
        term = gate * _dot(y.astype(BF16), wbr_ref[n])
        merged = term if merged is None else merged + term
    xo_ref[0] = x + _dot(merged.astype(BF16), wout_ref[...])


def _mix_call(x, y_b, gmix, win, conva, sgug, sgub, sguw, sgubias,
              confw, confg, confb, wgate, wbr, wout):
    b, s, d = x.shape
    bw = conva.shape[1]
    kd = confw.shape[0]
    ts = min(TS_MIX, s)
    nslab = bw // LANES
    tile = lambda w: pl.BlockSpec((1, ts, w), lambda i, j: (i, j, 0))
    consts = (gmix, win, conva, sgug, sgub, sguw, sgubias, confw, confg, confb,
              wgate, wbr, wout)
    return pl.pallas_call(
        functools.partial(_mix_body, ts=ts, bw=bw, kd=kd),
        out_shape=jax.ShapeDtypeStruct((b, s, d), F32),
        grid=(b, s // ts),
        in_specs=[tile(d), tile(bw)] + [_const_spec(c.shape) for c in consts],
        out_specs=tile(d),
        scratch_shapes=[pltpu.VMEM((nslab, HALO_A + ts, LANES), F32),
                        pltpu.VMEM((nslab, HALO_D + ts, LANES), F32)],
        compiler_params=_params(2),
        name="mix_merge",
    )(x, y_b, *consts)


def _ffn_body(x_ref, p_ref, gffn_ref, w1_ref, w2_ref, gple_ref, wpg_ref, wpp_ref,
              gnext_ref, *rest, ts, fh, bw, dils, final):
    if final:
        xo_ref, hid_ref = rest
    else:
        wqkv_ref, xo_ref, *qkv_refs, stage, hid_ref = rest
    x = x_ref[0]
    hb = _rms(x, gffn_ref[...]).astype(BF16)
    for c in range(fh // FFN_CHUNK):
        c0 = c * FFN_CHUNK
        gate = _dot(hb, w1_ref[:, c0:c0 + FFN_CHUNK])
        up = _dot(hb, w1_ref[:, fh + c0:fh + c0 + FFN_CHUNK])
        hid_ref[:, c0:c0 + FFN_CHUNK] = (gate * _sigmoid(gate) * up).astype(BF16)
    x = x + _dot(hid_ref[...], w2_ref[...])
    h3 = _rms(x, gple_ref[...]).astype(BF16)
    x = x + _sigmoid(_dot(h3, wpg_ref[...])) * _dot(p_ref[0].astype(BF16), wpp_ref[...])
    if final:
        xo_ref[0] = _rms(x, gnext_ref[...])
    else:
        xo_ref[0] = x
        _emit_qkv(x, gnext_ref, wqkv_ref, qkv_refs, stage, ts=ts, bw=bw, dils=dils)


def _ffn_call(x, p, gffn, w1, w2, gple, wpg, wpp, gnext, wqkv, dils, bw):
    b, s, d = x.shape
    fh = w2.shape[0]
    final = wqkv is None
    ts = min(TS_FFN, s)
    tile = lambda w: pl.BlockSpec((1, ts, w), lambda i, j: (i, j, 0))
    consts = (gffn, w1, w2, gple, wpg, wpp, gnext) + (() if final else (wqkv,))
    out_shape = [jax.ShapeDtypeStruct((b, s, d), F32)]
    out_specs = [tile(d)]
    scratch = [pltpu.VMEM((ts, fh), BF16)]
    if not final:
        qs, qspecs = _qkv_outs(b, s, ts, bw, dils)
        out_shape += qs
        out_specs += qspecs
        scratch = [pltpu.VMEM((3 * bw // LANES, ts, LANES), F32)] + scratch
    outs = pl.pallas_call(
        functools.partial(_ffn_body, ts=ts, fh=fh, bw=bw, dils=dils, final=final),
        out_shape=tuple(out_shape),
        grid=(b, s // ts),
        in_specs=[tile(d), tile(p.shape[-1])] + [_const_spec(c.shape) for c in consts],
        out_specs=tuple(out_specs),
        scratch_shapes=scratch,
        compiler_params=_params(2),
        name="ffn_ple",
    )(x, p, *consts)
    return outs[0], outs[1:]


def kernel(x, p, g_mix, w_in, conv_a, sgu_ln_g, sgu_ln_b, sgu_w, sgu_b, conf_dw,
           conf_ln_g, conf_ln_b, w_branch, w_merge_gate, w_out, g_ffn, w_ffn_in,
           w_ffn_out, g_ple, w_ple_gate, w_ple_proj, g_final):
    depth = w_in.shape[0]
    d = x.shape[-1]
    bw = d // N_BRANCH
    assert all(w // dil == ATT_BLK for w, dil in DSW_GROUPS)
    dils = tuple(dil for _, dil in DSW_GROUPS)
    row = lambda a: a.reshape(1, -1)
    blk = ATT_BLK
    head_of_col = jnp.arange(bw, dtype=jnp.int32) // (bw // ATT_HEADS)
    head_of_row = jnp.arange(ATT_HEADS * blk, dtype=jnp.int32) // blk
    hm = (head_of_row[:, None] == head_of_col[None, :]).astype(BF16)
    qi = (jnp.arange(ATT_HEADS * blk, dtype=jnp.int32) % blk)[:, None]
    ki = jnp.arange(2 * blk, dtype=jnp.int32)[None, :]
    bias = jnp.where((ki >= qi) & (ki <= qi + blk), 0.0, NEG).astype(F32)
    w_in_b = w_in.astype(BF16)
    w_qkv = w_in_b[:, :, 3 * bw:6 * bw]
    qkv = _qkv_call(x, row(g_mix[0]), w_qkv[0], dils)
    for i in range(depth):
        final = i == depth - 1
        y_b = _attn_call(qkv, hm, bias, dils, bw)
        win_rest = jnp.concatenate([w_in_b[i, :, :3 * bw], w_in_b[i, :, 6 * bw:]], axis=1)
        sgubias = jnp.repeat(sgu_b[i].T, bw // SGU_GROUPS, axis=1)
        x = _mix_call(
            x, y_b, row(g_mix[i]), win_rest,
            conv_a[i], row(sgu_ln_g[i]), row(sgu_ln_b[i]),
            sgu_w[i].reshape(SGU_GROUPS * SGU_CHUNK, SGU_CHUNK), sgubias,
            conf_dw[i], row(conf_ln_g[i]), row(conf_ln_b[i]),
            w_merge_gate[i].astype(BF16), w_branch[i].astype(BF16), w_out[i].astype(BF16))
        gnext = g_final if final else g_mix[i + 1]
        x, qkv = _ffn_call(
            x, p[i], row(g_ffn[i]), w_ffn_in[i].astype(BF16), w_ffn_out[i].astype(BF16),
            row(g_ple[i]), w_ple_gate[i].astype(BF16), w_ple_proj[i].astype(BF16),
            row(gnext), None if final else w_qkv[i + 1], dils, bw)
    return x
```

```python
import functools

import jax
import jax.numpy as jnp
from jax import lax
from jax.experimental import pallas as pl
from jax.experimental.pallas import tpu as pltpu

F32 = jnp.float32
BF16 = jnp.bfloat16

EPS = 1e-6
N_BRANCH = 4
ATT_HEADS = 4
DSW_GROUPS = ((128, 1), (512, 4), (2048, 16))
ATT_BLK = 128
ATT_SLOTS = 4
SGU_CHUNK = 128
SGU_GROUPS = 4
LANES = 128
HALO_A = 8
HALO_D = 32
CONV_ROWS = 64
MERGE_ROWS = 256
FFN_CHUNK = 256
NEG = -1e30
LOG2E = 1.4426950408889634
VMEM_LIMIT = 56 * 1024 * 1024

TS_QKV = 1024
TS_MIX = 512
TS_FFN = 512


def _rms(x, g):
    ms = jnp.mean(x * x, axis=-1, keepdims=True)
    return x * lax.rsqrt(ms + EPS) * g


def _ln(x, g, b):
    mu = jnp.mean(x, axis=-1, keepdims=True)
    xc = x - mu
    var = jnp.mean(xc * xc, axis=-1, keepdims=True)
    return xc * lax.rsqrt(var + EPS) * g + b


def _sigmoid(x):
    return 1.0 / (1.0 + jnp.exp(-x))


def _dot(a, b):
    return jnp.dot(a, b, preferred_element_type=F32)


def _after(x, anchor):
    z = pltpu.bitcast(anchor[-8:, -LANES:], jnp.uint32)
    z = lax.shift_right_logical(lax.shift_right_logical(z, jnp.uint32(16)), jnp.uint32(16))
    xi = pltpu.bitcast(x, jnp.uint32) + jnp.tile(z, (x.shape[0] // 8, x.shape[1] // LANES))
    return pltpu.bitcast(xi, F32)


def _const_spec(shape):
    nd = len(shape)
    return pl.BlockSpec(shape, lambda *_: (0,) * nd, pipeline_mode=pl.Buffered(1))


def _layer_spec(arr, layer):
    nd = arr.ndim - 1
    return pl.BlockSpec((None,) + arr.shape[1:], lambda *_: (layer,) + (0,) * nd,
                        pipeline_mode=pl.Buffered(1))


def _params(n_grid):
    return pltpu.CompilerParams(
        dimension_semantics=("arbitrary",) * n_grid,
        vmem_limit_bytes=VMEM_LIMIT)


def _emit_qkv(x, g_ref, win_ref, outs, stages, *, ts, bw, dils):
    h = _rms(x, g_ref[...]).astype(BF16)
    qkv = _dot(h, win_ref[:, 3 * bw:6 * bw])
    nsl = bw // LANES
    scale = float((bw // ATT_HEADS) ** -0.5 * LOG2E)
    for n, dil in enumerate(dils):
        last = n == len(dils) - 1
        for c in range(3 * nsl):
            t, sl = divmod(c, nsl)
            for r in range(dil):
                if n == 0:
                    rows = qkv[:, c * LANES:(c + 1) * LANES]
                    rows = rows * scale if t == 0 else rows
                else:
                    ratio = dil // dils[n - 1]
                    rows = stages[n - 1][c, r % dils[n - 1],
                                         pl.ds(r // dils[n - 1], ts // dil, stride=ratio), :]
                if not last:
                    stages[n][c, r] = rows
                lo = r * bw + sl * LANES
                outs[3 * n + t][0, :, lo:lo + LANES] = rows.astype(BF16)


def _qkv_outs(b, s, ts, bw, dils):
    assert all(b_ % a_ == 0 for a_, b_ in zip(dils, dils[1:])) and dils[0] == 1
    out_shape, out_specs = [], []
    for dil in dils:
        for _ in range(3):
            out_shape.append(jax.ShapeDtypeStruct((b, s // dil, dil * bw), BF16))
            out_specs.append(pl.BlockSpec((1, ts // dil, dil * bw), lambda i, j: (i, j, 0)))
    scratch = [pltpu.VMEM((3 * bw // LANES, dil, ts // dil, LANES), F32) for dil in dils[:-1]]
    return out_shape, out_specs, scratch


def _qkv_body(x_ref, g_ref, win_ref, *rest, ts, bw, dils):
    n_out = 3 * len(dils)
    _emit_qkv(x_ref[0], g_ref, win_ref, rest[:n_out], rest[n_out:], ts=ts, bw=bw, dils=dils)


def _qkv_call(x, g, win, layer, dils, bw):
    b, s, d = x.shape
    ts = min(TS_QKV, s)
    out_shape, out_specs, scratch = _qkv_outs(b, s, ts, bw, dils)
    return pl.pallas_call(
        functools.partial(_qkv_body, ts=ts, bw=bw, dils=dils),
        out_shape=tuple(out_shape),
        grid=(b, s // ts),
        in_specs=[pl.BlockSpec((1, ts, d), lambda i, j: (i, j, 0)),
                  _layer_spec(g, layer), _layer_spec(win, layer)],
        out_specs=tuple(out_specs),
        scratch_shapes=scratch,
        compiler_params=_params(2),
        name="qkv_proj",
    )(x, g, win)


def _attn_block(q, kk, vv, hm, bias, first, bw):
    blk = ATT_BLK
    nh = ATT_HEADS
    dh = bw // nh
    hps = LANES // dh
    qst = jnp.concatenate([q] * nh, axis=0) * hm
    s = lax.dot_general(qst, kk, (((1,), (1,)), ((), ())), preferred_element_type=F32)
    s = s + bias
    if first is not False:
        s = jnp.concatenate([jnp.where(first, NEG, s[:, :blk]), s[:, blk:]], axis=1)
    m = jnp.max(s, axis=1, keepdims=True)
    p = jnp.exp2(s - m).astype(BF16)
    ones = jnp.ones((2 * blk, LANES), BF16)
    head_in_slab = lax.broadcasted_iota(jnp.int32, (blk, LANES), 1) // dh
    out = []
    for sl in range(bw // LANES):
        r0 = sl * hps * blk
        rhs = jnp.concatenate([vv[:, sl * LANES:(sl + 1) * LANES], ones], axis=1)
        ol = _dot(p[r0:r0 + hps * blk], rhs)
        o_s, l_s = ol[0:blk, :LANES], ol[0:blk, LANES:]
        m_s = jnp.broadcast_to(m[r0:r0 + blk], (blk, LANES))
        for hh in range(1, hps):
            sel = head_in_slab == hh
            rows = slice(hh * blk, (hh + 1) * blk)
            o_s = jnp.where(sel, ol[rows, :LANES], o_s)
            l_s = jnp.where(sel, ol[rows, LANES:], l_s)
            m_s = jnp.where(sel, m[r0 + hh * blk:r0 + (hh + 1) * blk], m_s)
        out.append((o_s, m_s, l_s))
    return out


def _attn_tiling(dil, span):
    rows = max(ATT_SLOTS // dil, 1)
    cols = min(dil, ATT_SLOTS)
    row_steps = span // (ATT_BLK * dil) // rows
    col_steps = dil // cols
    return rows, cols, row_steps, col_steps


def _attn_body(*refs, bw, dils, span):
    ng = len(dils)
    hm_ref, bias_ref, yb_ref, onat, mnat, lnat = refs[5 * ng:]
    blk = ATT_BLK
    nsl = bw // LANES
    j = pl.program_id(1)
    it = pl.program_id(2)
    hm = hm_ref[...]
    bias = bias_ref[...]
    for g, dil in enumerate(dils):
        q_ref, kc_ref, kp_ref, vc_ref, vp_ref = refs[5 * g:5 * g + 5]
        rows, cols, _, col_steps = _attn_tiling(dil, span)
        row_step = it // col_steps if col_steps > 1 else it
        col_step = it % col_steps if col_steps > 1 else 0
        for a in range(rows):
            for c in range(cols):
                cs = slice(c * bw, (c + 1) * bw)
                q = q_ref[0, a * blk:(a + 1) * blk, cs]
                if a == 0:
                    kk = jnp.concatenate([kp_ref[0, :, cs], kc_ref[0, 0:blk, cs]], axis=0)
                    vv = jnp.concatenate([vp_ref[0, :, cs], vc_ref[0, 0:blk, cs]], axis=0)
                    first = (j == 0) & (row_step == 0)
                else:
                    kk = kc_ref[0, (a - 1) * blk:(a + 1) * blk, cs]
                    vv = vc_ref[0, (a - 1) * blk:(a + 1) * blk, cs]
                    first = False
                stats = _attn_block(q, kk, vv, hm, bias, first, bw)
                start = (row_step * rows + a) * (blk * dil) + col_step * cols + c
                if dil == 1:
                    idx = pl.ds(pl.multiple_of(start, blk), blk)
                else:
                    idx = pl.ds(start, blk, stride=dil)
                for sl, (o_s, m_s, l_s) in enumerate(stats):
                    onat[g, sl, idx, :] = o_s
                    mnat[g, sl, idx, :] = m_s
                    lnat[g, sl, idx, :] = l_s

    @pl.when(it == pl.num_programs(2) - 1)
    def _():
        def merge(c, carry):
            rows = pl.ds(pl.multiple_of(c * MERGE_ROWS, MERGE_ROWS), MERGE_ROWS)
            for sl in range(nsl):
                ms = [mnat[g, sl, rows, :] for g in range(ng)]
                mm = functools.reduce(jnp.maximum, ms)
                ws = [jnp.exp2(m - mm) for m in ms]
                num = sum(w * onat[g, sl, rows, :] for g, w in enumerate(ws))
                den = sum(w * lnat[g, sl, rows, :] for g, w in enumerate(ws))
                yb_ref[0, rows, sl * LANES:(sl + 1) * LANES] = (num / den).astype(BF16)
            return carry
        lax.fori_loop(0, span // MERGE_ROWS, merge, 0)


def _attn_call(qkv, hm, bias, dils, bw):
    b, s, _ = qkv[0].shape
    blk = ATT_BLK
    span = blk * max(dils)
    nsteps = span // blk // ATT_SLOTS
    in_specs, args = [], []
    for g, dil in enumerate(dils):
        rows, cols, row_steps, col_steps = _attn_tiling(dil, span)
        assert row_steps * col_steps == nsteps

        def cur(i, j, it, rs=row_steps, cst=col_steps):
            return (i, j * rs + it // cst, it % cst)

        def prev(i, j, it, rs=row_steps, cst=col_steps, r=rows):
            return (i, jnp.maximum((j * rs + it // cst) * r - 1, 0), it % cst)

        q, k, v = qkv[3 * g:3 * g + 3]
        cur_spec = pl.BlockSpec((1, rows * blk, cols * bw), cur)
        prev_spec = pl.BlockSpec((1, blk, cols * bw), prev)
        in_specs += [cur_spec, cur_spec, prev_spec, cur_spec, prev_spec]
        args += [q, k, k, v, v]
    nsl = bw // LANES
    return pl.pallas_call(
        functools.partial(_attn_body, bw=bw, dils=dils, span=span),
        out_shape=jax.ShapeDtypeStruct((b, s, bw), BF16),
        grid=(b, s // span, nsteps),
        in_specs=in_specs + [_const_spec(hm.shape), _const_spec(bias.shape)],
        out_specs=pl.BlockSpec((1, span, bw), lambda i, j, u: (i, j, 0)),
        scratch_shapes=[pltpu.VMEM((len(dils), nsl, span, LANES), F32)] * 3,
        compiler_params=_params(3),
        name="band_attn",
    )(*args, hm, bias)


def _mix_body(x_ref, yb_ref,
              gmix_ref, win_ref, conva_ref, sgug_ref, sgub_ref, sguw_ref, sgubias_ref,
              confw_ref, confg_ref, confb_ref, wgate_ref, wbr_ref, wout_ref,
              xo_ref, exta, extd, *, ts, bw, kd):
    nslab = bw // LANES

    @pl.when(pl.program_id(1) == 0)
    def _():
        exta[:, 0:HALO_A, :] = jnp.zeros((nslab, HALO_A, LANES), F32)
        extd[:, 0:HALO_D, :] = jnp.zeros((nslab, HALO_D, LANES), F32)

    x = x_ref[0]
    hb = _rms(x, gmix_ref[...]).astype(BF16)
    a_b, a_c, a_x = (_dot(hb, win_ref[:, n * bw:(n + 1) * bw]) for n in range(3))
    s_u, s_v, c_val, c_gate = (_dot(hb, win_ref[:, n * bw:(n + 1) * bw]) for n in range(6, 10))
    gates = [_sigmoid(_dot(hb, wgate_ref[n])) for n in range(N_BRANCH)]

    ca = a_c * a_x
    ka = conva_ref.shape[0]
    ya = []
    for sl in range(nslab):
        ls = slice(sl * LANES, (sl + 1) * LANES)
        exta[sl, HALO_A:HALO_A + ts, :] = ca[:, ls]
        acc = None
        for t in range(ka):
            off = HALO_A - (ka - 1) + t
            term = conva_ref[t:t + 1, ls] * exta[sl, off:off + ts, :]
            acc = term if acc is None else acc + term
        ya.append(a_b[:, ls] * acc)
        exta[sl, 0:HALO_A, :] = exta[sl, ts:ts + HALO_A, :]
    y_a = jnp.concatenate(ya, axis=1)

    vb = _ln(s_v, sgug_ref[...], sgub_ref[...]).astype(BF16)
    ck = SGU_CHUNK
    wr = lax.broadcasted_iota(jnp.int32, (SGU_GROUPS * ck, ck), 0) & (ck - 1)
    wc = lax.broadcasted_iota(jnp.int32, (SGU_GROUPS * ck, ck), 1)
    wst = jnp.where(wc <= wr, sguw_ref[...], 0.0).astype(BF16)
    group_of_lane = lax.broadcasted_iota(jnp.int32, (ck, bw), 1) // (bw // SGU_GROUPS)
    yc = []
    for c in range(ts // ck):
        mm = _dot(wst, vb[c * ck:(c + 1) * ck, :])
        mixed = mm[0:ck]
        for g in range(1, SGU_GROUPS):
            mixed = jnp.where(group_of_lane == g, mm[g * ck:(g + 1) * ck], mixed)
        yc.append(s_u[c * ck:(c + 1) * ck, :] * (mixed + sgubias_ref[...]))
    y_c = jnp.concatenate(yc, axis=0)

    glu = c_val * _sigmoid(c_gate)
    for sl in range(nslab):
        extd[sl, HALO_D:HALO_D + ts, :] = glu[:, sl * LANES:(sl + 1) * LANES]
    cd = []
    n_pieces = nslab * (ts // CONV_ROWS)
    for sl in range(nslab):
        ls = slice(sl * LANES, (sl + 1) * LANES)
        pieces = []
        for r0 in range(0, ts, CONV_ROWS):
            acc = None
            for t in range(kd):
                off = r0 + HALO_D - (kd - 1) + t
                term = confw_ref[t:t + 1, ls] * extd[sl, off:off + CONV_ROWS, :]
                acc = term if acc is None else acc + term
            pi = len(pieces) + sl * (ts // CONV_ROWS)
            if pi % (n_pieces // N_BRANCH) == 0 and pi > 0:
                acc = _after(acc, gates[pi // (n_pieces // N_BRANCH) - 1])
            pieces.append(acc)
        cd.append(jnp.concatenate(pieces, axis=0))
        extd[sl, 0:HALO_D, :] = extd[sl, ts:ts + HALO_D, :]
    cd = _ln(jnp.concatenate(cd, axis=1), confg_ref[...], confb_ref[...])
    y_d = cd * _sigmoid(cd)

    merged = None
    for n, y in enumerate((y_a, yb_ref[0], y_c, y_d)):
        term = gates[n] * _dot(y.astype(BF16), wbr_ref[n])
        merged = term if merged is None else merged + term
    xo_ref[0] = x + _dot(merged.astype(BF16), wout_ref[...])


def _mix_call(x, y_b, layer, gmix, win, conva, sgug, sgub, sguw, sgubias,
              confw, confg, confb, wgate, wbr, wout):
    b, s, d = x.shape
    bw = conva.shape[-1]
    kd = confw.shape[1]
    ts = min(TS_MIX, s)
    nslab = bw // LANES
    tile = lambda w: pl.BlockSpec((1, ts, w), lambda i, j: (i, j, 0))
    consts = (gmix, win, conva, sgug, sgub, sguw, sgubias, confw, confg, confb,
              wgate, wbr, wout)
    return pl.pallas_call(
        functools.partial(_mix_body, ts=ts, bw=bw, kd=kd),
        out_shape=jax.ShapeDtypeStruct((b, s, d), F32),
        grid=(b, s // ts),
        in_specs=[tile(d), tile(bw)] + [_layer_spec(c, layer) for c in consts],
        out_specs=tile(d),
        scratch_shapes=[pltpu.VMEM((nslab, HALO_A + ts, LANES), F32),
                        pltpu.VMEM((nslab, HALO_D + ts, LANES), F32)],
        compiler_params=_params(2),
        name="mix_merge",
    )(x, y_b, *consts)


def _ffn_body(x_ref, p_ref, gffn_ref, w1_ref, w2_ref, gple_ref, wpg_ref, wpp_ref,
              gnext_ref, *rest, ts, fh, bw, dils, final):
    if final:
        xo_ref, hid_ref = rest
    else:
        n_out = 3 * len(dils)
        win_ref, xo_ref = rest[:2]
        qkv_refs, stages, hid_ref = rest[2:2 + n_out], rest[2 + n_out:-1], rest[-1]
    x = x_ref[0]
    hb = _rms(x, gffn_ref[...]).astype(BF16)
    for c in range(fh // FFN_CHUNK):
        c0 = c * FFN_CHUNK
        gate = _dot(hb, w1_ref[:, c0:c0 + FFN_CHUNK])
        up = _dot(hb, w1_ref[:, fh + c0:fh + c0 + FFN_CHUNK])
        hid_ref[:, c0:c0 + FFN_CHUNK] = (gate * _sigmoid(gate) * up).astype(BF16)
    x = x + _dot(hid_ref[...], w2_ref[...])
    h3 = _rms(x, gple_ref[...]).astype(BF16)
    x = x + _sigmoid(_dot(h3, wpg_ref[...])) * _dot(p_ref[0].astype(BF16), wpp_ref[...])
    if final:
        xo_ref[0] = _rms(x, gnext_ref[...])
    else:
        xo_ref[0] = x
        _emit_qkv(x, gnext_ref, win_ref, qkv_refs, stages, ts=ts, bw=bw, dils=dils)


def _ffn_call(x, p, layer, gffn, w1, w2, gple, wpg, wpp, gnext, win, dils, bw):
    b, s, d = x.shape
    fh = w2.shape[1]
    final = win is None
    ts = min(TS_FFN, s)
    tile = lambda w: pl.BlockSpec((1, ts, w), lambda i, j: (i, j, 0))
    in_specs = [tile(d),
                pl.BlockSpec((None, 1, ts, p.shape[-1]), lambda i, j: (layer, i, j, 0))]
    in_specs += [_layer_spec(c, layer) for c in (gffn, w1, w2, gple, wpg, wpp)]
    out_shape = [jax.ShapeDtypeStruct((b, s, d), F32)]
    out_specs = [tile(d)]
    scratch = [pltpu.VMEM((ts, fh), BF16)]
    if final:
        consts = (gnext,)
        in_specs.append(_layer_spec(gnext, 0))
    else:
        consts = (gnext, win)
        in_specs += [_layer_spec(gnext, layer + 1), _layer_spec(win, layer + 1)]
        qs, qspecs, qscratch = _qkv_outs(b, s, ts, bw, dils)
        out_shape += qs
        out_specs += qspecs
        scratch = qscratch + scratch
    outs = pl.pallas_call(
        functools.partial(_ffn_body, ts=ts, fh=fh, bw=bw, dils=dils, final=final),
        out_shape=tuple(out_shape),
        grid=(b, s // ts),
        in_specs=in_specs,
        out_specs=tuple(out_specs),
        scratch_shapes=scratch,
        compiler_params=_params(2),
        name="ffn_ple",
    )(x, p, gffn, w1, w2, gple, wpg, wpp, *consts)
    return outs[0], outs[1:]


def kernel(x, p, g_mix, w_in, conv_a, sgu_ln_g, sgu_ln_b, sgu_w, sgu_b, conf_dw,
           conf_ln_g, conf_ln_b, w_branch, w_merge_gate, w_out, g_ffn, w_ffn_in,
           w_ffn_out, g_ple, w_ple_gate, w_ple_proj, g_final):
    depth = w_in.shape[0]
    d = x.shape[-1]
    bw = d // N_BRANCH
    assert all(w // dil == ATT_BLK for w, dil in DSW_GROUPS)
    dils = tuple(dil for _, dil in DSW_GROUPS)
    blk = ATT_BLK
    head_of_col = jnp.arange(bw, dtype=jnp.int32) // (bw // ATT_HEADS)
    head_of_row = jnp.arange(ATT_HEADS * blk, dtype=jnp.int32) // blk
    hm = (head_of_row[:, None] == head_of_col[None, :]).astype(BF16)
    qi = (jnp.arange(ATT_HEADS * blk, dtype=jnp.int32) % blk)[:, None]
    ki = jnp.arange(2 * blk, dtype=jnp.int32)[None, :]
    bias = jnp.where((ki >= qi) & (ki <= qi + blk), 0.0, NEG).astype(F32)

    rows = lambda a: a.reshape(a.shape[0], 1, a.shape[-1])
    g_mix, sgu_ln_g, sgu_ln_b, conf_ln_g, conf_ln_b, g_ffn, g_ple = map(
        rows, (g_mix, sgu_ln_g, sgu_ln_b, conf_ln_g, conf_ln_b, g_ffn, g_ple))
    g_final = g_final.reshape(1, 1, d)
    sgu_w = sgu_w.reshape(depth, SGU_GROUPS * SGU_CHUNK, SGU_CHUNK)
    sgubias = jnp.repeat(jnp.swapaxes(sgu_b, 1, 2), bw // SGU_GROUPS, axis=2)
    w_in, w_merge_gate, w_branch, w_out, w_ffn_in, w_ffn_out, w_ple_gate, w_ple_proj = (
        w.astype(BF16) for w in (w_in, w_merge_gate, w_branch, w_out, w_ffn_in, w_ffn_out,
                                 w_ple_gate, w_ple_proj))

    qkv = _qkv_call(x, g_mix, w_in, 0, dils, bw)
    for i in range(depth):
        final = i == depth - 1
        y_b = _attn_call(qkv, hm, bias, dils, bw)
        x = _mix_call(x, y_b, i, g_mix, w_in, conv_a, sgu_ln_g, sgu_ln_b, sgu_w, sgubias,
                      conf_dw, conf_ln_g, conf_ln_b, w_merge_gate, w_branch, w_out)
        x, qkv = _ffn_call(x, p, i, g_ffn, w_ffn_in, w_ffn_out, g_ple, w_ple_gate, w_ple_proj,
                           g_final if final else g_mix, None if final else w_in, dils, bw)
    return x
```

```python
import functools

import jax
import jax.numpy as jnp
from jax import lax
from jax.experimental import pallas as pl
from jax.experimental.pallas import tpu as pltpu

F32 = jnp.float32
BF16 = jnp.bfloat16

EPS = 1e-6
N_BRANCH = 4
ATT_HEADS = 4
DSW_GROUPS = ((128, 1), (512, 4), (2048, 16))
ATT_BLK = 128
ATT_SLOTS = 8
SGU_CHUNK = 128
SGU_GROUPS = 4
LANES = 128
HALO_A = 8
HALO_D = 32
CONV_ROWS = 64
MERGE_ROWS = 256
FFN_CHUNK = 256
FFN_SUB = 2
NEG = -1e30
LOG2E = 1.4426950408889634
VMEM_LIMIT = 56 * 1024 * 1024

TS_QKV = 1024
TS_MIX = 512
TS_FFN = 512


def _rms(x, g):
    ms = jnp.mean(x * x, axis=-1, keepdims=True)
    return x * lax.rsqrt(ms + EPS) * g


def _ln(x, g, b):
    mu = jnp.mean(x, axis=-1, keepdims=True)
    xc = x - mu
    var = jnp.mean(xc * xc, axis=-1, keepdims=True)
    return xc * lax.rsqrt(var + EPS) * g + b


def _sigmoid(x):
    return 1.0 / (1.0 + jnp.exp(-x))


def _dot(a, b):
    return jnp.dot(a, b, preferred_element_type=F32)


def _after(x, anchor):
    z = pltpu.bitcast(anchor[-8:, -LANES:], jnp.uint32)
    z = lax.shift_right_logical(lax.shift_right_logical(z, jnp.uint32(16)), jnp.uint32(16))
    xi = pltpu.bitcast(x, jnp.uint32) + jnp.tile(z, (x.shape[0] // 8, x.shape[1] // LANES))
    return pltpu.bitcast(xi, F32)


def _const_spec(shape):
    nd = len(shape)
    return pl.BlockSpec(shape, lambda *_: (0,) * nd, pipeline_mode=pl.Buffered(1))


def _layer_spec(arr, layer):
    nd = arr.ndim - 1
    return pl.BlockSpec((None,) + arr.shape[1:], lambda *_: (layer,) + (0,) * nd,
                        pipeline_mode=pl.Buffered(1))


def _params(n_grid):
    return pltpu.CompilerParams(
        dimension_semantics=("arbitrary",) * n_grid,
        vmem_limit_bytes=VMEM_LIMIT)


def _emit_qkv(x, g_ref, win_ref, outs, stages, *, ts, bw, dils):
    h = _rms(x, g_ref[...]).astype(BF16)
    qkv = _dot(h, win_ref[:, 3 * bw:6 * bw])
    nsl = bw // LANES
    scale = float((bw // ATT_HEADS) ** -0.5 * LOG2E)
    for n, dil in enumerate(dils):
        last = n == len(dils) - 1
        for c in range(3 * nsl):
            t, sl = divmod(c, nsl)
            for r in range(dil):
                if n == 0:
                    rows = qkv[:, c * LANES:(c + 1) * LANES]
                    rows = rows * scale if t == 0 else rows
                else:
                    ratio = dil // dils[n - 1]
                    rows = stages[n - 1][c, r % dils[n - 1],
                                         pl.ds(r // dils[n - 1], ts // dil, stride=ratio), :]
                if not last:
                    stages[n][c, r] = rows
                lo = r * bw + sl * LANES
                outs[3 * n + t][0, :, lo:lo + LANES] = rows.astype(BF16)


def _qkv_outs(b, s, ts, bw, dils):
    assert all(b_ % a_ == 0 for a_, b_ in zip(dils, dils[1:])) and dils[0] == 1
    out_shape, out_specs = [], []
    for dil in dils:
        for _ in range(3):
            out_shape.append(jax.ShapeDtypeStruct((b, s // dil, dil * bw), BF16))
            out_specs.append(pl.BlockSpec((1, ts // dil, dil * bw), lambda i, j: (i, j, 0)))
    scratch = [pltpu.VMEM((3 * bw // LANES, dil, ts // dil, LANES), F32) for dil in dils[:-1]]
    return out_shape, out_specs, scratch


def _qkv_body(x_ref, g_ref, win_ref, *rest, ts, bw, dils):
    n_out = 3 * len(dils)
    _emit_qkv(x_ref[0], g_ref, win_ref, rest[:n_out], rest[n_out:], ts=ts, bw=bw, dils=dils)


def _qkv_call(x, g, win, layer, dils, bw):
    b, s, d = x.shape
    ts = min(TS_QKV, s)
    out_shape, out_specs, scratch = _qkv_outs(b, s, ts, bw, dils)
    return pl.pallas_call(
        functools.partial(_qkv_body, ts=ts, bw=bw, dils=dils),
        out_shape=tuple(out_shape),
        grid=(b, s // ts),
        in_specs=[pl.BlockSpec((1, ts, d), lambda i, j: (i, j, 0)),
                  _layer_spec(g, layer), _layer_spec(win, layer)],
        out_specs=tuple(out_specs),
        scratch_shapes=scratch,
        compiler_params=_params(2),
        name="qkv_proj",
    )(x, g, win)


def _attn_block(q, kk, vv, hm, bias, first, bw):
    blk = ATT_BLK
    nh = ATT_HEADS
    dh = bw // nh
    hps = LANES // dh
    qst = jnp.concatenate([q] * nh, axis=0) * hm
    s = lax.dot_general(qst, kk, (((1,), (1,)), ((), ())), preferred_element_type=F32)
    s = s + bias
    if first is not False:
        s = jnp.concatenate([jnp.where(first, NEG, s[:, :blk]), s[:, blk:]], axis=1)
    m = jnp.max(s, axis=1, keepdims=True)
    p = jnp.exp2(s - m).astype(BF16)
    ones = jnp.ones((2 * blk, LANES), BF16)
    head_in_slab = lax.broadcasted_iota(jnp.int32, (blk, LANES), 1) // dh
    out = []
    for sl in range(bw // LANES):
        r0 = sl * hps * blk
        rhs = jnp.concatenate([vv[:, sl * LANES:(sl + 1) * LANES], ones], axis=1)
        ol = _dot(p[r0:r0 + hps * blk], rhs)
        o_s, l_s = ol[0:blk, :LANES], ol[0:blk, LANES:]
        m_s = jnp.broadcast_to(m[r0:r0 + blk], (blk, LANES))
        for hh in range(1, hps):
            sel = head_in_slab == hh
            rows = slice(hh * blk, (hh + 1) * blk)
            o_s = jnp.where(sel, ol[rows, :LANES], o_s)
            l_s = jnp.where(sel, ol[rows, LANES:], l_s)
            m_s = jnp.where(sel, m[r0 + hh * blk:r0 + (hh + 1) * blk], m_s)
        out.append((o_s, m_s, l_s))
    return out


def _attn_tiling(dil, span):
    rows = max(ATT_SLOTS // dil, 1)
    cols = min(dil, ATT_SLOTS)
    row_steps = span // (ATT_BLK * dil) // rows
    col_steps = dil // cols
    return rows, cols, row_steps, col_steps


def _attn_body(*refs, bw, dils, span):
    ng = len(dils)
    hm_ref, bias_ref, yb_ref, onat, mnat, lnat = refs[5 * ng:]
    blk = ATT_BLK
    nsl = bw // LANES
    j = pl.program_id(1)
    it = pl.program_id(2)
    hm = hm_ref[...]
    bias = bias_ref[...]
    for g, dil in enumerate(dils):
        q_ref, kc_ref, kp_ref, vc_ref, vp_ref = refs[5 * g:5 * g + 5]
        rows, cols, _, col_steps = _attn_tiling(dil, span)
        row_step = it // col_steps if col_steps > 1 else it
        col_step = it % col_steps if col_steps > 1 else 0
        for a in range(rows):
            for c in range(cols):
                cs = slice(c * bw, (c + 1) * bw)
                q = q_ref[0, a * blk:(a + 1) * blk, cs]
                if a == 0:
                    kk = jnp.concatenate([kp_ref[0, :, cs], kc_ref[0, 0:blk, cs]], axis=0)
                    vv = jnp.concatenate([vp_ref[0, :, cs], vc_ref[0, 0:blk, cs]], axis=0)
                    first = (j == 0) & (row_step == 0)
                else:
                    kk = kc_ref[0, (a - 1) * blk:(a + 1) * blk, cs]
                    vv = vc_ref[0, (a - 1) * blk:(a + 1) * blk, cs]
                    first = False
                stats = _attn_block(q, kk, vv, hm, bias, first, bw)
                start = (row_step * rows + a) * (blk * dil) + col_step * cols + c
                if dil == 1:
                    idx = pl.ds(pl.multiple_of(start, blk), blk)
                else:
                    idx = pl.ds(start, blk, stride=dil)
                for sl, (o_s, m_s, l_s) in enumerate(stats):
                    onat[g, sl, idx, :] = o_s
                    mnat[g, sl, idx, :] = m_s
                    lnat[g, sl, idx, :] = l_s

    @pl.when(it == pl.num_programs(2) - 1)
    def _():
        def merge(c, carry):
            rows = pl.ds(pl.multiple_of(c * MERGE_ROWS, MERGE_ROWS), MERGE_ROWS)
            for sl in range(nsl):
                ms = [mnat[g, sl, rows, :] for g in range(ng)]
                mm = functools.reduce(jnp.maximum, ms)
                ws = [jnp.exp2(m - mm) for m in ms]
                num = sum(w * onat[g, sl, rows, :] for g, w in enumerate(ws))
                den = sum(w * lnat[g, sl, rows, :] for g, w in enumerate(ws))
                yb_ref[0, rows, sl * LANES:(sl + 1) * LANES] = (num / den).astype(BF16)
            return carry
        lax.fori_loop(0, span // MERGE_ROWS, merge, 0)


def _attn_call(qkv, hm, bias, dils, bw):
    b, s, _ = qkv[0].shape
    blk = ATT_BLK
    span = blk * max(dils)
    nsteps = span // blk // ATT_SLOTS
    in_specs, args = [], []
    for g, dil in enumerate(dils):
        rows, cols, row_steps, col_steps = _attn_tiling(dil, span)
        assert row_steps * col_steps == nsteps

        def cur(i, j, it, rs=row_steps, cst=col_steps):
            return (i, j * rs + it // cst, it % cst)

        def prev(i, j, it, rs=row_steps, cst=col_steps, r=rows):
            return (i, jnp.maximum((j * rs + it // cst) * r - 1, 0), it % cst)

        q, k, v = qkv[3 * g:3 * g + 3]
        cur_spec = pl.BlockSpec((1, rows * blk, cols * bw), cur)
        prev_spec = pl.BlockSpec((1, blk, cols * bw), prev)
        in_specs += [cur_spec, cur_spec, prev_spec, cur_spec, prev_spec]
        args += [q, k, k, v, v]
    nsl = bw // LANES
    return pl.pallas_call(
        functools.partial(_attn_body, bw=bw, dils=dils, span=span),
        out_shape=jax.ShapeDtypeStruct((b, s, bw), BF16),
        grid=(b, s // span, nsteps),
        in_specs=in_specs + [_const_spec(hm.shape), _const_spec(bias.shape)],
        out_specs=pl.BlockSpec((1, span, bw), lambda i, j, u: (i, j, 0)),
        scratch_shapes=[pltpu.VMEM((len(dils), nsl, span, LANES), F32)] * 3,
        compiler_params=_params(3),
        name="band_attn",
    )(*args, hm, bias)


def _mix_body(x_ref, yb_ref,
              gmix_ref, win_ref, conva_ref, sgug_ref, sgub_ref, sguw_ref, sgubias_ref,
              confw_ref, confg_ref, confb_ref, wgate_ref, wbr_ref, wout_ref,
              xo_ref, exta, extd, *, ts, bw, kd):
    nslab = bw // LANES

    @pl.when(pl.program_id(1) == 0)
    def _():
        exta[:, 0:HALO_A, :] = jnp.zeros((nslab, HALO_A, LANES), F32)
        extd[:, 0:HALO_D, :] = jnp.zeros((nslab, HALO_D, LANES), F32)

    x = x_ref[0]
    hb = _rms(x, gmix_ref[...]).astype(BF16)
    a_b, a_c, a_x = (_dot(hb, win_ref[:, n * bw:(n + 1) * bw]) for n in range(3))
    s_u, s_v, c_val, c_gate = (_dot(hb, win_ref[:, n * bw:(n + 1) * bw]) for n in range(6, 10))
    gates = [_sigmoid(_dot(hb, wgate_ref[n])) for n in range(N_BRANCH)]

    ca = a_c * a_x
    ka = conva_ref.shape[0]
    ya = []
    for sl in range(nslab):
        ls = slice(sl * LANES, (sl + 1) * LANES)
        exta[sl, HALO_A:HALO_A + ts, :] = ca[:, ls]
        acc = None
        for t in range(ka):
            off = HALO_A - (ka - 1) + t
            term = conva_ref[t:t + 1, ls] * exta[sl, off:off + ts, :]
            acc = term if acc is None else acc + term
        ya.append(a_b[:, ls] * acc)
        exta[sl, 0:HALO_A, :] = exta[sl, ts:ts + HALO_A, :]
    y_a = jnp.concatenate(ya, axis=1)

    vb = _ln(s_v, sgug_ref[...], sgub_ref[...]).astype(BF16)
    ck = SGU_CHUNK
    wr = lax.broadcasted_iota(jnp.int32, (SGU_GROUPS * ck, ck), 0) & (ck - 1)
    wc = lax.broadcasted_iota(jnp.int32, (SGU_GROUPS * ck, ck), 1)
    wst = jnp.where(wc <= wr, sguw_ref[...], 0.0).astype(BF16)
    group_of_lane = lax.broadcasted_iota(jnp.int32, (ck, bw), 1) // (bw // SGU_GROUPS)
    yc = []
    for c in range(ts // ck):
        mm = _dot(wst, vb[c * ck:(c + 1) * ck, :])
        mixed = mm[0:ck]
        for g in range(1, SGU_GROUPS):
            mixed = jnp.where(group_of_lane == g, mm[g * ck:(g + 1) * ck], mixed)
        yc.append(s_u[c * ck:(c + 1) * ck, :] * (mixed + sgubias_ref[...]))
    y_c = jnp.concatenate(yc, axis=0)

    glu = c_val * _sigmoid(c_gate)
    for sl in range(nslab):
        extd[sl, HALO_D:HALO_D + ts, :] = glu[:, sl * LANES:(sl + 1) * LANES]
    cd = []
    n_pieces = nslab * (ts // CONV_ROWS)
    for sl in range(nslab):
        ls = slice(sl * LANES, (sl + 1) * LANES)
        pieces = []
        for r0 in range(0, ts, CONV_ROWS):
            acc = None
            for t in range(kd):
                off = r0 + HALO_D - (kd - 1) + t
                term = confw_ref[t:t + 1, ls] * extd[sl, off:off + CONV_ROWS, :]
                acc = term if acc is None else acc + term
            pi = len(pieces) + sl * (ts // CONV_ROWS)
            if pi % (n_pieces // N_BRANCH) == 0 and pi > 0:
                acc = _after(acc, gates[pi // (n_pieces // N_BRANCH) - 1])
            pieces.append(acc)
        cd.append(jnp.concatenate(pieces, axis=0))
        extd[sl, 0:HALO_D, :] = extd[sl, ts:ts + HALO_D, :]
    cd = _ln(jnp.concatenate(cd, axis=1), confg_ref[...], confb_ref[...])
    y_d = cd * _sigmoid(cd)

    merged = None
    for n, y in enumerate((y_a, yb_ref[0], y_c, y_d)):
        term = gates[n] * _dot(y.astype(BF16), wbr_ref[n])
        merged = term if merged is None else merged + term
    xo_ref[0] = x + _dot(merged.astype(BF16), wout_ref[...])


def _mix_call(x, y_b, layer, gmix, win, conva, sgug, sgub, sguw, sgubias,
              confw, confg, confb, wgate, wbr, wout):
    b, s, d = x.shape
    bw = conva.shape[-1]
    kd = confw.shape[1]
    ts = min(TS_MIX, s)
    nslab = bw // LANES
    tile = lambda w: pl.BlockSpec((1, ts, w), lambda i, j: (i, j, 0))
    consts = (gmix, win, conva, sgug, sgub, sguw, sgubias, confw, confg, confb,
              wgate, wbr, wout)
    return pl.pallas_call(
        functools.partial(_mix_body, ts=ts, bw=bw, kd=kd),
        out_shape=jax.ShapeDtypeStruct((b, s, d), F32),
        grid=(b, s // ts),
        in_specs=[tile(d), tile(bw)] + [_layer_spec(c, layer) for c in consts],
        out_specs=tile(d),
        scratch_shapes=[pltpu.VMEM((nslab, HALO_A + ts, LANES), F32),
                        pltpu.VMEM((nslab, HALO_D + ts, LANES), F32)],
        compiler_params=_params(2),
        name="mix_merge",
    )(x, y_b, *consts)


def _ffn_body(x_ref, p_ref, gffn_ref, w1_ref, w2_ref, gple_ref, wpg_ref, wpp_ref,
              gnext_ref, *rest, ts, fh, bw, dils, final):
    if final:
        xo_ref, hid_ref = rest
    else:
        n_out = 3 * len(dils)
        win_ref, xo_ref = rest[:2]
        qkv_refs, stages, hid_ref = rest[2:2 + n_out], rest[2 + n_out:-1], rest[-1]
    rs = ts // FFN_SUB
    subs = [slice(h * rs, (h + 1) * rs) for h in range(FFN_SUB)]
    xs = [x_ref[0, r, :] for r in subs]
    hbs = [_rms(x, gffn_ref[...]).astype(BF16) for x in xs]
    for c in range(fh // FFN_CHUNK):
        c0 = c * FFN_CHUNK
        for r, hb in zip(subs, hbs):
            gate = _dot(hb, w1_ref[:, c0:c0 + FFN_CHUNK])
            up = _dot(hb, w1_ref[:, fh + c0:fh + c0 + FFN_CHUNK])
            hid_ref[r, c0:c0 + FFN_CHUNK] = (gate * _sigmoid(gate) * up).astype(BF16)
    xs = [x + _dot(hid_ref[r, :], w2_ref[...]) for r, x in zip(subs, xs)]
    h3s = [_rms(x, gple_ref[...]).astype(BF16) for x in xs]
    xs = [x + _sigmoid(_dot(h3, wpg_ref[...])) * _dot(p_ref[0, r, :].astype(BF16), wpp_ref[...])
          for r, x, h3 in zip(subs, xs, h3s)]
    x = jnp.concatenate(xs, axis=0)
    if final:
        xo_ref[0] = _rms(x, gnext_ref[...])
    else:
        xo_ref[0] = x
        _emit_qkv(x, gnext_ref, win_ref, qkv_refs, stages, ts=ts, bw=bw, dils=dils)


def _ffn_call(x, p, layer, gffn, w1, w2, gple, wpg, wpp, gnext, win, dils, bw):
    b, s, d = x.shape
    fh = w2.shape[1]
    final = win is None
    ts = min(TS_FFN, s)
    tile = lambda w: pl.BlockSpec((1, ts, w), lambda i, j: (i, j, 0))
    in_specs = [tile(d),
                pl.BlockSpec((None, 1, ts, p.shape[-1]), lambda i, j: (layer, i, j, 0))]
    in_specs += [_layer_spec(c, layer) for c in (gffn, w1, w2, gple, wpg, wpp)]
    out_shape = [jax.ShapeDtypeStruct((b, s, d), F32)]
    out_specs = [tile(d)]
    scratch = [pltpu.VMEM((ts, fh), BF16)]
    if final:
        consts = (gnext,)
        in_specs.append(_layer_spec(gnext, 0))
    else:
        consts = (gnext, win)
        in_specs += [_layer_spec(gnext, layer + 1), _layer_spec(win, layer + 1)]
        qs, qspecs, qscratch = _qkv_outs(b, s, ts, bw, dils)
        out_shape += qs
        out_specs += qspecs
        scratch = qscratch + scratch
    outs = pl.pallas_call(
        functools.partial(_ffn_body, ts=ts, fh=fh, bw=bw, dils=dils, final=final),
        out_shape=tuple(out_shape),
        grid=(b, s // ts),
        in_specs=in_specs,
        out_specs=tuple(out_specs),
        scratch_shapes=scratch,
        compiler_params=_params(2),
        name="ffn_ple",
    )(x, p, gffn, w1, w2, gple, wpg, wpp, *consts)
    return outs[0], outs[1:]


def kernel(x, p, g_mix, w_in, conv_a, sgu_ln_g, sgu_ln_b, sgu_w, sgu_b, conf_dw,
           conf_ln_g, conf_ln_b, w_branch, w_merge_gate, w_out, g_ffn, w_ffn_in,
           w_ffn_out, g_ple, w_ple_gate, w_ple_proj, g_final):
    depth = w_in.shape[0]
    d = x.shape[-1]
    bw = d // N_BRANCH
    assert all(w // dil == ATT_BLK for w, dil in DSW_GROUPS)
    dils = tuple(dil for _, dil in DSW_GROUPS)
    blk = ATT_BLK
    head_of_col = jnp.arange(bw, dtype=jnp.int32) // (bw // ATT_HEADS)
    head_of_row = jnp.arange(ATT_HEADS * blk, dtype=jnp.int32) // blk
    hm = (head_of_row[:, None] == head_of_col[None, :]).astype(BF16)
    qi = (jnp.arange(ATT_HEADS * blk, dtype=jnp.int32) % blk)[:, None]
    ki = jnp.arange(2 * blk, dtype=jnp.int32)[None, :]
    bias = jnp.where((ki >= qi) & (ki <= qi + blk), 0.0, NEG).astype(F32)

    rows = lambda a: a.reshape(a.shape[0], 1, a.shape[-1])
    g_mix, sgu_ln_g, sgu_ln_b, conf_ln_g, conf_ln_b, g_ffn, g_ple = map(
        rows, (g_mix, sgu_ln_g, sgu_ln_b, conf_ln_g, conf_ln_b, g_ffn, g_ple))
    g_final = g_final.reshape(1, 1, d)
    sgu_w = sgu_w.reshape(depth, SGU_GROUPS * SGU_CHUNK, SGU_CHUNK)
    sgubias = jnp.repeat(jnp.swapaxes(sgu_b, 1, 2), bw // SGU_GROUPS, axis=2)
    w_in, w_merge_gate, w_branch, w_out, w_ffn_in, w_ffn_out, w_ple_gate, w_ple_proj = (
        w.astype(BF16) for w in (w_in, w_merge_gate, w_branch, w_out, w_ffn_in, w_ffn_out,
                                 w_ple_gate, w_ple_proj))

    qkv = _qkv_call(x, g_mix, w_in, 0, dils, bw)
    for i in range(depth):
        final = i == depth - 1
        y_b = _attn_call(qkv, hm, bias, dils, bw)
        x = _mix_call(x, y_b, i, g_mix, w_in, conv_a, sgu_ln_g, sgu_ln_b, sgu_w, sgubias,
                      conf_dw, conf_ln_g, conf_ln_b, w_merge_gate, w_branch, w_out)
        x, qkv = _ffn_call(x, p, i, g_ffn, w_ffn_in, w_ffn_out, g_ple, w_ple_gate, w_ple_proj,
                           g_final if final else g_mix, None if final else w_in, dils, bw)
    return x
```

```python
import functools

import jax
import jax.numpy as jnp
from jax import lax
from jax.experimental import pallas as pl
from jax.experimental.pallas import tpu as pltpu

F32 = jnp.float32
BF16 = jnp.bfloat16

EPS = 1e-6
N_BRANCH = 4
ATT_HEADS = 4
DSW_GROUPS = ((128, 1), (512, 4), (2048, 16))
ATT_BLK = 128
ATT_SLOTS = 8
SGU_CHUNK = 128
SGU_GROUPS = 4
LANES = 128
HALO_A = 8
HALO_D = 32
CONV_ROWS = 64
MERGE_ROWS = 256
FFN_CHUNK = 256
FFN_SUB = 2
QKV_SUB = 2
NEG = -1e30
LOG2E = 1.4426950408889634
VMEM_LIMIT = 56 * 1024 * 1024

TS_QKV = 1024
TS_MIX = 512
TS_FFN = 512


def _rms(x, g):
    ms = jnp.mean(x * x, axis=-1, keepdims=True)
    return x * lax.rsqrt(ms + EPS) * g


def _ln(x, g, b):
    mu = jnp.mean(x, axis=-1, keepdims=True)
    xc = x - mu
    var = jnp.mean(xc * xc, axis=-1, keepdims=True)
    return xc * lax.rsqrt(var + EPS) * g + b


def _sigmoid(x):
    return 1.0 / (1.0 + jnp.exp(-x))


def _dot(a, b):
    return jnp.dot(a, b, preferred_element_type=F32)


def _after_head(x, anchor):
    head = pltpu.bitcast(x[:16, :LANES], jnp.uint32)
    z = pltpu.bitcast(anchor[-8:, -LANES:], jnp.uint32)
    z = lax.shift_right_logical(lax.shift_right_logical(z, jnp.uint32(16)), jnp.uint32(16))
    head = pltpu.bitcast(head + z, BF16)
    top = jnp.concatenate([head, x[:16, LANES:]], axis=1)
    return jnp.concatenate([top, x[16:]], axis=0)


def _const_spec(shape):
    nd = len(shape)
    return pl.BlockSpec(shape, lambda *_: (0,) * nd, pipeline_mode=pl.Buffered(1))


def _layer_spec(arr, layer):
    nd = arr.ndim - 1
    return pl.BlockSpec((None,) + arr.shape[1:], lambda *_: (layer,) + (0,) * nd,
                        pipeline_mode=pl.Buffered(1))


def _params(n_grid):
    return pltpu.CompilerParams(
        dimension_semantics=("arbitrary",) * n_grid,
        vmem_limit_bytes=VMEM_LIMIT)


def _emit_qkv(x, g_ref, win_ref, outs, stages, *, row0, bw, dils):
    rs = x.shape[0]
    h = _rms(x, g_ref[...]).astype(BF16)
    qkv = _dot(h, win_ref[:, 3 * bw:6 * bw])
    nsl = bw // LANES
    scale = float((bw // ATT_HEADS) ** -0.5 * LOG2E)
    for n, dil in enumerate(dils):
        last = n == len(dils) - 1
        dst = slice(row0 // dil, (row0 + rs) // dil)
        for c in range(3 * nsl):
            t, sl = divmod(c, nsl)
            for r in range(dil):
                if n == 0:
                    rows = qkv[:, c * LANES:(c + 1) * LANES]
                    rows = rows * scale if t == 0 else rows
                else:
                    prev = dils[n - 1]
                    rows = stages[n - 1][c, r % prev,
                                         pl.ds(row0 // prev + r // prev, rs // dil,
                                               stride=dil // prev), :]
                if not last:
                    stages[n][c, r, dst, :] = rows
                lo = r * bw + sl * LANES
                outs[3 * n + t][0, dst, lo:lo + LANES] = rows.astype(BF16)


def _qkv_outs(b, s, ts, bw, dils):
    assert all(b_ % a_ == 0 for a_, b_ in zip(dils, dils[1:])) and dils[0] == 1
    out_shape, out_specs = [], []
    for dil in dils:
        for _ in range(3):
            out_shape.append(jax.ShapeDtypeStruct((b, s // dil, dil * bw), BF16))
            out_specs.append(pl.BlockSpec((1, ts // dil, dil * bw), lambda i, j: (i, j, 0)))
    scratch = [pltpu.VMEM((3 * bw // LANES, dil, ts // dil, LANES), F32) for dil in dils[:-1]]
    return out_shape, out_specs, scratch


def _qkv_body(x_ref, g_ref, win_ref, *rest, ts, bw, dils):
    n_out = 3 * len(dils)
    rs = ts // QKV_SUB
    for h in range(QKV_SUB):
        _emit_qkv(x_ref[0, h * rs:(h + 1) * rs, :], g_ref, win_ref, rest[:n_out], rest[n_out:],
                  row0=h * rs, bw=bw, dils=dils)


def _qkv_call(x, g, win, layer, dils, bw):
    b, s, d = x.shape
    ts = min(TS_QKV, s)
    out_shape, out_specs, scratch = _qkv_outs(b, s, ts, bw, dils)
    return pl.pallas_call(
        functools.partial(_qkv_body, ts=ts, bw=bw, dils=dils),
        out_shape=tuple(out_shape),
        grid=(b, s // ts),
        in_specs=[pl.BlockSpec((1, ts, d), lambda i, j: (i, j, 0)),
                  _layer_spec(g, layer), _layer_spec(win, layer)],
        out_specs=tuple(out_specs),
        scratch_shapes=scratch,
        compiler_params=_params(2),
        name="qkv_proj",
    )(x, g, win)


def _attn_block(q, kk, vv, hm, bias, first, bw):
    blk = ATT_BLK
    nh = ATT_HEADS
    dh = bw // nh
    hps = LANES // dh
    qst = jnp.concatenate([q] * nh, axis=0) * hm
    s = lax.dot_general(qst, kk, (((1,), (1,)), ((), ())), preferred_element_type=F32)
    s = s + bias
    if first is not False:
        s = jnp.concatenate([jnp.where(first, NEG, s[:, :blk]), s[:, blk:]], axis=1)
    m = jnp.max(s, axis=1, keepdims=True)
    p = jnp.exp2(s - m).astype(BF16)
    ones = jnp.ones((2 * blk, LANES), BF16)
    head_in_slab = lax.broadcasted_iota(jnp.int32, (blk, LANES), 1) // dh
    out = []
    for sl in range(bw // LANES):
        r0 = sl * hps * blk
        rhs = jnp.concatenate([vv[:, sl * LANES:(sl + 1) * LANES], ones], axis=1)
        ol = _dot(p[r0:r0 + hps * blk], rhs)
        o_s, l_s = ol[0:blk, :LANES], ol[0:blk, LANES:]
        m_s = jnp.broadcast_to(m[r0:r0 + blk], (blk, LANES))
        for hh in range(1, hps):
            sel = head_in_slab == hh
            rows = slice(hh * blk, (hh + 1) * blk)
            o_s = jnp.where(sel, ol[rows, :LANES], o_s)
            l_s = jnp.where(sel, ol[rows, LANES:], l_s)
            m_s = jnp.where(sel, m[r0 + hh * blk:r0 + (hh + 1) * blk], m_s)
        out.append((o_s, m_s, l_s))
    return out


def _attn_tiling(dil, span):
    rows = max(ATT_SLOTS // dil, 1)
    cols = min(dil, ATT_SLOTS)
    row_steps = span // (ATT_BLK * dil) // rows
    col_steps = dil // cols
    return rows, cols, row_steps, col_steps


def _attn_body(*refs, bw, dils, span):
    ng = len(dils)
    hm_ref, bias_ref, yb_ref, onat, mnat, lnat = refs[5 * ng:]
    blk = ATT_BLK
    nsl = bw // LANES
    j = pl.program_id(1)
    it = pl.program_id(2)
    hm = hm_ref[...]
    bias = bias_ref[...]
    for g, dil in enumerate(dils):
        q_ref, kc_ref, kp_ref, vc_ref, vp_ref = refs[5 * g:5 * g + 5]
        rows, cols, _, col_steps = _attn_tiling(dil, span)
        row_step = it // col_steps if col_steps > 1 else it
        col_step = it % col_steps if col_steps > 1 else 0
        for a in range(rows):
            for c in range(cols):
                cs = slice(c * bw, (c + 1) * bw)
                q = q_ref[0, a * blk:(a + 1) * blk, cs]
                if a == 0:
                    kk = jnp.concatenate([kp_ref[0, :, cs], kc_ref[0, 0:blk, cs]], axis=0)
                    vv = jnp.concatenate([vp_ref[0, :, cs], vc_ref[0, 0:blk, cs]], axis=0)
                    first = (j == 0) & (row_step == 0)
                else:
                    kk = kc_ref[0, (a - 1) * blk:(a + 1) * blk, cs]
                    vv = vc_ref[0, (a - 1) * blk:(a + 1) * blk, cs]
                    first = False
                stats = _attn_block(q, kk, vv, hm, bias, first, bw)
                start = (row_step * rows + a) * (blk * dil) + col_step * cols + c
                if dil == 1:
                    idx = pl.ds(pl.multiple_of(start, blk), blk)
                else:
                    idx = pl.ds(start, blk, stride=dil)
                for sl, (o_s, m_s, l_s) in enumerate(stats):
                    onat[g, sl, idx, :] = o_s
                    mnat[g, sl, idx, :] = m_s
                    lnat[g, sl, idx, :] = l_s

    @pl.when(it == pl.num_programs(2) - 1)
    def _():
        def merge(c, carry):
            rows = pl.ds(pl.multiple_of(c * MERGE_ROWS, MERGE_ROWS), MERGE_ROWS)
            for sl in range(nsl):
                ms = [mnat[g, sl, rows, :] for g in range(ng)]
                mm = functools.reduce(jnp.maximum, ms)
                ws = [jnp.exp2(m - mm) for m in ms]
                num = sum(w * onat[g, sl, rows, :] for g, w in enumerate(ws))
                den = sum(w * lnat[g, sl, rows, :] for g, w in enumerate(ws))
                yb_ref[0, rows, sl * LANES:(sl + 1) * LANES] = (num / den).astype(BF16)
            return carry
        lax.fori_loop(0, span // MERGE_ROWS, merge, 0)


def _attn_call(qkv, hm, bias, dils, bw):
    b, s, _ = qkv[0].shape
    blk = ATT_BLK
    span = blk * max(dils)
    nsteps = span // blk // ATT_SLOTS
    in_specs, args = [], []
    for g, dil in enumerate(dils):
        rows, cols, row_steps, col_steps = _attn_tiling(dil, span)
        assert row_steps * col_steps == nsteps

        def cur(i, j, it, rs=row_steps, cst=col_steps):
            return (i, j * rs + it // cst, it % cst)

        def prev(i, j, it, rs=row_steps, cst=col_steps, r=rows):
            return (i, jnp.maximum((j * rs + it // cst) * r - 1, 0), it % cst)

        q, k, v = qkv[3 * g:3 * g + 3]
        cur_spec = pl.BlockSpec((1, rows * blk, cols * bw), cur)
        prev_spec = pl.BlockSpec((1, blk, cols * bw), prev)
        in_specs += [cur_spec, cur_spec, prev_spec, cur_spec, prev_spec]
        args += [q, k, k, v, v]
    nsl = bw // LANES
    return pl.pallas_call(
        functools.partial(_attn_body, bw=bw, dils=dils, span=span),
        out_shape=jax.ShapeDtypeStruct((b, s, bw), BF16),
        grid=(b, s // span, nsteps),
        in_specs=in_specs + [_const_spec(hm.shape), _const_spec(bias.shape)],
        out_specs=pl.BlockSpec((1, span, bw), lambda i, j, u: (i, j, 0)),
        scratch_shapes=[pltpu.VMEM((len(dils), nsl, span, LANES), F32)] * 3,
        compiler_params=_params(3),
        name="band_attn",
    )(*args, hm, bias)


def _mix_body(x_ref, yb_ref,
              gmix_ref, win_ref, conva_ref, sgug_ref, sgub_ref, sguw_ref, sgubias_ref,
              confw_ref, confg_ref, confb_ref, wgate_ref, wbr_ref, wout_ref,
              xo_ref, exta, extd, *, ts, bw, kd):
    nslab = bw // LANES

    @pl.when(pl.program_id(1) == 0)
    def _():
        exta[:, 0:HALO_A, :] = jnp.zeros((nslab, HALO_A, LANES), F32)
        extd[:, 0:HALO_D, :] = jnp.zeros((nslab, HALO_D, LANES), F32)

    x = x_ref[0]
    hb = _rms(x, gmix_ref[...]).astype(BF16)
    proj = lambda n: _dot(hb, win_ref[:, n * bw:(n + 1) * bw])
    c_val, c_gate = proj(8), proj(9)

    glu = c_val * _sigmoid(c_gate)
    for sl in range(nslab):
        extd[sl, HALO_D:HALO_D + ts, :] = glu[:, sl * LANES:(sl + 1) * LANES]
    a_b, a_c, a_x, s_u, s_v = proj(0), proj(1), proj(2), proj(6), proj(7)
    gates = [_sigmoid(_dot(hb, wgate_ref[0]))]
    per_slab = ts // CONV_ROWS
    n_pieces = nslab * per_slab
    pieces = []
    for pi in range(n_pieces):
        sl, r0 = pi // per_slab, (pi % per_slab) * CONV_ROWS
        ls = slice(sl * LANES, (sl + 1) * LANES)
        acc = None
        for t in range(kd):
            off = r0 + HALO_D - (kd - 1) + t
            term = confw_ref[t:t + 1, ls] * extd[sl, off:off + CONV_ROWS, :]
            acc = term if acc is None else acc + term
        pieces.append(acc)
        if (pi + 1) % (n_pieces // N_BRANCH) == 0 and len(gates) < N_BRANCH:
            gates.append(_sigmoid(_dot(_after_head(hb, acc), wgate_ref[len(gates)])))
    for sl in range(nslab):
        extd[sl, 0:HALO_D, :] = extd[sl, ts:ts + HALO_D, :]
    cd = jnp.concatenate([jnp.concatenate(pieces[sl * per_slab:(sl + 1) * per_slab], axis=0)
                          for sl in range(nslab)], axis=1)
    cd = _ln(cd, confg_ref[...], confb_ref[...])
    y_d = cd * _sigmoid(cd)

    ca = a_c * a_x
    ka = conva_ref.shape[0]
    ya = []
    for sl in range(nslab):
        ls = slice(sl * LANES, (sl + 1) * LANES)
        exta[sl, HALO_A:HALO_A + ts, :] = ca[:, ls]
        acc = None
        for t in range(ka):
            off = HALO_A - (ka - 1) + t
            term = conva_ref[t:t + 1, ls] * exta[sl, off:off + ts, :]
            acc = term if acc is None else acc + term
        ya.append(a_b[:, ls] * acc)
        exta[sl, 0:HALO_A, :] = exta[sl, ts:ts + HALO_A, :]
    y_a = jnp.concatenate(ya, axis=1)

    vb = _ln(s_v, sgug_ref[...], sgub_ref[...]).astype(BF16)
    ck = SGU_CHUNK
    wr = lax.broadcasted_iota(jnp.int32, (SGU_GROUPS * ck, ck), 0) & (ck - 1)
    wc = lax.broadcasted_iota(jnp.int32, (SGU_GROUPS * ck, ck), 1)
    wst = jnp.where(wc <= wr, sguw_ref[...], 0.0).astype(BF16)
    group_of_lane = lax.broadcasted_iota(jnp.int32, (ck, bw), 1) // (bw // SGU_GROUPS)
    yc = []
    for c in range(ts // ck):
        mm = _dot(wst, vb[c * ck:(c + 1) * ck, :])
        mixed = mm[0:ck]
        for g in range(1, SGU_GROUPS):
            mixed = jnp.where(group_of_lane == g, mm[g * ck:(g + 1) * ck], mixed)
        yc.append(s_u[c * ck:(c + 1) * ck, :] * (mixed + sgubias_ref[...]))
    y_c = jnp.concatenate(yc, axis=0)

    merged = None
    for n, y in enumerate((y_a, yb_ref[0], y_c, y_d)):
        term = gates[n] * _dot(y.astype(BF16), wbr_ref[n])
        merged = term if merged is None else merged + term
    xo_ref[0] = x + _dot(merged.astype(BF16), wout_ref[...])


def _mix_call(x, y_b, layer, gmix, win, conva, sgug, sgub, sguw, sgubias,
              confw, confg, confb, wgate, wbr, wout):
    b, s, d = x.shape
    bw = conva.shape[-1]
    kd = confw.shape[1]
    ts = min(TS_MIX, s)
    nslab = bw // LANES
    tile = lambda w: pl.BlockSpec((1, ts, w), lambda i, j: (i, j, 0))
    consts = (gmix, win, conva, sgug, sgub, sguw, sgubias, confw, confg, confb,
              wgate, wbr, wout)
    return pl.pallas_call(
        functools.partial(_mix_body, ts=ts, bw=bw, kd=kd),
        out_shape=jax.ShapeDtypeStruct((b, s, d), F32),
        grid=(b, s // ts),
        in_specs=[tile(d), tile(bw)] + [_layer_spec(c, layer) for c in consts],
        out_specs=tile(d),
        scratch_shapes=[pltpu.VMEM((nslab, HALO_A + ts, LANES), F32),
                        pltpu.VMEM((nslab, HALO_D + ts, LANES), F32)],
        compiler_params=_params(2),
        name="mix_merge",
    )(x, y_b, *consts)


def _ffn_body(x_ref, p_ref, gffn_ref, w1_ref, w2_ref, gple_ref, wpg_ref, wpp_ref,
              gnext_ref, *rest, ts, fh, bw, dils, final):
    if final:
        xo_ref, hid_ref = rest
    else:
        n_out = 3 * len(dils)
        win_ref, xo_ref = rest[:2]
        qkv_refs, stages, hid_ref = rest[2:2 + n_out], rest[2 + n_out:-1], rest[-1]
    rs = ts // FFN_SUB
    subs = [slice(h * rs, (h + 1) * rs) for h in range(FFN_SUB)]
    xs = [x_ref[0, r, :] for r in subs]
    hbs = [_rms(x, gffn_ref[...]).astype(BF16) for x in xs]
    for c in range(fh // FFN_CHUNK):
        c0 = c * FFN_CHUNK
        for r, hb in zip(subs, hbs):
            gate = _dot(hb, w1_ref[:, c0:c0 + FFN_CHUNK])
            up = _dot(hb, w1_ref[:, fh + c0:fh + c0 + FFN_CHUNK])
            hid_ref[r, c0:c0 + FFN_CHUNK] = (gate * _sigmoid(gate) * up).astype(BF16)
    xs = [x + _dot(hid_ref[r, :], w2_ref[...]) for r, x in zip(subs, xs)]
    h3s = [_rms(x, gple_ref[...]).astype(BF16) for x in xs]
    xs = [x + _sigmoid(_dot(h3, wpg_ref[...])) * _dot(p_ref[0, r, :].astype(BF16), wpp_ref[...])
          for r, x, h3 in zip(subs, xs, h3s)]
    for r, x in zip(subs, xs):
        if final:
            xo_ref[0, r, :] = _rms(x, gnext_ref[...])
        else:
            xo_ref[0, r, :] = x
            _emit_qkv(x, gnext_ref, win_ref, qkv_refs, stages, row0=r.start, bw=bw, dils=dils)


def _ffn_call(x, p, layer, gffn, w1, w2, gple, wpg, wpp, gnext, win, dils, bw):
    b, s, d = x.shape
    fh = w2.shape[1]
    final = win is None
    ts = min(TS_FFN, s)
    tile = lambda w: pl.BlockSpec((1, ts, w), lambda i, j: (i, j, 0))
    in_specs = [tile(d),
                pl.BlockSpec((None, 1, ts, p.shape[-1]), lambda i, j: (layer, i, j, 0))]
    in_specs += [_layer_spec(c, layer) for c in (gffn, w1, w2, gple, wpg, wpp)]
    out_shape = [jax.ShapeDtypeStruct((b, s, d), F32)]
    out_specs = [tile(d)]
    scratch = [pltpu.VMEM((ts, fh), BF16)]
    if final:
        consts = (gnext,)
        in_specs.append(_layer_spec(gnext, 0))
    else:
        consts = (gnext, win)
        in_specs += [_layer_spec(gnext, layer + 1), _layer_spec(win, layer + 1)]
        qs, qspecs, qscratch = _qkv_outs(b, s, ts, bw, dils)
        out_shape += qs
        out_specs += qspecs
        scratch = qscratch + scratch
    outs = pl.pallas_call(
        functools.partial(_ffn_body, ts=ts, fh=fh, bw=bw, dils=dils, final=final),
        out_shape=tuple(out_shape),
        grid=(b, s // ts),
        in_specs=in_specs,
        out_specs=tuple(out_specs),
        scratch_shapes=scratch,
        compiler_params=_params(2),
        name="ffn_ple",
    )(x, p, gffn, w1, w2, gple, wpg, wpp, *consts)
    return outs[0], outs[1:]


def kernel(x, p, g_mix, w_in, conv_a, sgu_ln_g, sgu_ln_b, sgu_w, sgu_b, conf_dw,
           conf_ln_g, conf_ln_b, w_branch, w_merge_gate, w_out, g_ffn, w_ffn_in,
           w_ffn_out, g_ple, w_ple_gate, w_ple_proj, g_final):
    depth = w_in.shape[0]
    d = x.shape[-1]
    bw = d // N_BRANCH
    assert all(w // dil == ATT_BLK for w, dil in DSW_GROUPS)
    dils = tuple(dil for _, dil in DSW_GROUPS)
    blk = ATT_BLK
    head_of_col = jnp.arange(bw, dtype=jnp.int32) // (bw // ATT_HEADS)
    head_of_row = jnp.arange(ATT_HEADS * blk, dtype=jnp.int32) // blk
    hm = (head_of_row[:, None] == head_of_col[None, :]).astype(BF16)
    qi = (jnp.arange(ATT_HEADS * blk, dtype=jnp.int32) % blk)[:, None]
    ki = jnp.arange(2 * blk, dtype=jnp.int32)[None, :]
    bias = jnp.where((ki >= qi) & (ki <= qi + blk), 0.0, NEG).astype(F32)

    rows = lambda a: a.reshape(a.shape[0], 1, a.shape[-1])
    g_mix, sgu_ln_g, sgu_ln_b, conf_ln_g, conf_ln_b, g_ffn, g_ple = map(
        rows, (g_mix, sgu_ln_g, sgu_ln_b, conf_ln_g, conf_ln_b, g_ffn, g_ple))
    g_final = g_final.reshape(1, 1, d)
    sgu_w = sgu_w.reshape(depth, SGU_GROUPS * SGU_CHUNK, SGU_CHUNK)
    sgubias = jnp.repeat(jnp.swapaxes(sgu_b, 1, 2), bw // SGU_GROUPS, axis=2)
    w_in, w_merge_gate, w_branch, w_out, w_ffn_in, w_ffn_out, w_ple_gate, w_ple_proj = (
        w.astype(BF16) for w in (w_in, w_merge_gate, w_branch, w_out, w_ffn_in, w_ffn_out,
                                 w_ple_gate, w_ple_proj))

    qkv = _qkv_call(x, g_mix, w_in, 0, dils, bw)
    for i in range(depth):
        final = i == depth - 1
        y_b = _attn_call(qkv, hm, bias, dils, bw)
        x = _mix_call(x, y_b, i, g_mix, w_in, conv_a, sgu_ln_g, sgu_ln_b, sgu_w, sgubias,
                      conf_dw, conf_ln_g, conf_ln_b, w_merge_gate, w_branch, w_out)
        x, qkv = _ffn_call(x, p, i, g_ffn, w_ffn_in, w_ffn_out, g_ple, w_ple_gate, w_ple_proj,
                           g_final if final else g_mix, None if final else w_in, dils, bw)
    return x
```

```python
import functools

import jax
import jax.numpy as jnp
from jax import lax
from jax.experimental import pallas as pl
from jax.experimental.pallas import tpu as pltpu

F32 = jnp.float32
BF16 = jnp.bfloat16

EPS = 1e-6
N_BRANCH = 4
ATT_HEADS = 4
DSW_GROUPS = ((128, 1), (512, 4), (2048, 16))
ATT_BLK = 128
ATT_SLOTS = 8
SGU_CHUNK = 128
SGU_GROUPS = 4
LANES = 128
HALO_A = 8
HALO_D = 32
CONV_ROWS = 64
MERGE_ROWS = 256
FFN_CHUNK = 256
FFN_SUB = 2
QKV_SUB = 2
NEG = -float("inf")
LOG2E = 1.4426950408889634
VMEM_LIMIT = 56 * 1024 * 1024

TS_QKV = 1024
TS_MIX = 512
TS_FFN = 512


def _rms(x, g):
    ms = jnp.mean(x * x, axis=-1, keepdims=True)
    return x * lax.rsqrt(ms + EPS) * g


def _ln(x, g, b):
    mu = jnp.mean(x, axis=-1, keepdims=True)
    xc = x - mu
    var = jnp.mean(xc * xc, axis=-1, keepdims=True)
    return xc * lax.rsqrt(var + EPS) * g + b


def _sigmoid(x):
    return 1.0 / (1.0 + jnp.exp(-x))


def _dot(a, b):
    return jnp.dot(a, b, preferred_element_type=F32)


def _after(x, anchor):
    z = pltpu.bitcast(anchor[-8:, -LANES:], jnp.uint32)
    z = lax.shift_right_logical(lax.shift_right_logical(z, jnp.uint32(16)), jnp.uint32(16))
    xi = pltpu.bitcast(x, jnp.uint32) + jnp.tile(z, (x.shape[0] // 8, x.shape[1] // LANES))
    return pltpu.bitcast(xi, F32)


def _const_spec(shape):
    nd = len(shape)
    return pl.BlockSpec(shape, lambda *_: (0,) * nd, pipeline_mode=pl.Buffered(1))


def _layer_spec(arr, layer):
    nd = arr.ndim - 1
    return pl.BlockSpec((None,) + arr.shape[1:], lambda *_: (layer,) + (0,) * nd,
                        pipeline_mode=pl.Buffered(1))


def _params(n_grid):
    return pltpu.CompilerParams(
        dimension_semantics=("arbitrary",) * n_grid,
        vmem_limit_bytes=VMEM_LIMIT)


def _emit_qkv(x, g_ref, win_ref, outs, stages, *, row0, bw, dils):
    rs = x.shape[0]
    h = _rms(x, g_ref[...]).astype(BF16)
    qkv = _dot(h, win_ref[:, 3 * bw:6 * bw])
    nsl = bw // LANES
    scale = float((bw // ATT_HEADS) ** -0.5 * LOG2E)
    for n, dil in enumerate(dils):
        last = n == len(dils) - 1
        dst = slice(row0 // dil, (row0 + rs) // dil)
        for c in range(3 * nsl):
            t, sl = divmod(c, nsl)
            for r in range(dil):
                if n == 0:
                    rows = qkv[:, c * LANES:(c + 1) * LANES]
                    rows = rows * scale if t == 0 else rows
                else:
                    prev = dils[n - 1]
                    rows = stages[n - 1][c, r % prev,
                                         pl.ds(row0 // prev + r // prev, rs // dil,
                                               stride=dil // prev), :]
                if not last:
                    stages[n][c, r, dst, :] = rows
                lo = r * bw + sl * LANES
                outs[3 * n + t][0, dst, lo:lo + LANES] = rows.astype(BF16)


def _qkv_outs(b, s, ts, bw, dils):
    assert all(b_ % a_ == 0 for a_, b_ in zip(dils, dils[1:])) and dils[0] == 1
    out_shape, out_specs = [], []
    for dil in dils:
        for _ in range(3):
            out_shape.append(jax.ShapeDtypeStruct((b, s // dil, dil * bw), BF16))
            out_specs.append(pl.BlockSpec((1, ts // dil, dil * bw), lambda i, j: (i, j, 0)))
    scratch = [pltpu.VMEM((3 * bw // LANES, dil, ts // dil, LANES), F32) for dil in dils[:-1]]
    return out_shape, out_specs, scratch


def _qkv_body(x_ref, g_ref, win_ref, *rest, ts, bw, dils):
    n_out = 3 * len(dils)
    rs = ts // QKV_SUB
    for h in range(QKV_SUB):
        _emit_qkv(x_ref[0, h * rs:(h + 1) * rs, :], g_ref, win_ref, rest[:n_out], rest[n_out:],
                  row0=h * rs, bw=bw, dils=dils)


def _qkv_call(x, g, win, layer, dils, bw):
    b, s, d = x.shape
    ts = min(TS_QKV, s)
    out_shape, out_specs, scratch = _qkv_outs(b, s, ts, bw, dils)
    return pl.pallas_call(
        functools.partial(_qkv_body, ts=ts, bw=bw, dils=dils),
        out_shape=tuple(out_shape),
        grid=(b, s // ts),
        in_specs=[pl.BlockSpec((1, ts, d), lambda i, j: (i, j, 0)),
                  _layer_spec(g, layer), _layer_spec(win, layer)],
        out_specs=tuple(out_specs),
        scratch_shapes=scratch,
        compiler_params=_params(2),
        name="qkv_proj",
    )(x, g, win)


def _attn_block(q, kk, vv, hm, bias, first, bw):
    blk = ATT_BLK
    nh = ATT_HEADS
    dh = bw // nh
    hps = LANES // dh
    qst = jnp.concatenate([q] * nh, axis=0) * hm
    s = lax.dot_general(qst, kk, (((1,), (1,)), ((), ())), preferred_element_type=F32)
    s = s + bias
    if first is not False:
        s = jnp.concatenate([jnp.where(first, NEG, s[:, :blk]), s[:, blk:]], axis=1)
    m = jnp.max(s, axis=1, keepdims=True)
    p = jnp.exp2(s - m).astype(BF16)
    ones = jnp.ones((2 * blk, LANES), BF16)
    head_in_slab = lax.broadcasted_iota(jnp.int32, (blk, LANES), 1) // dh
    out = []
    for sl in range(bw // LANES):
        r0 = sl * hps * blk
        rhs = jnp.concatenate([vv[:, sl * LANES:(sl + 1) * LANES], ones], axis=1)
        ol = _dot(p[r0:r0 + hps * blk], rhs)
        o_s, l_s = ol[0:blk, :LANES], ol[0:blk, LANES:]
        m_s = jnp.broadcast_to(m[r0:r0 + blk], (blk, LANES))
        for hh in range(1, hps):
            sel = head_in_slab == hh
            rows = slice(hh * blk, (hh + 1) * blk)
            o_s = jnp.where(sel, ol[rows, :LANES], o_s)
            l_s = jnp.where(sel, ol[rows, LANES:], l_s)
            m_s = jnp.where(sel, m[r0 + hh * blk:r0 + (hh + 1) * blk], m_s)
        out.append((o_s, m_s, l_s))
    return out


def _attn_tiling(dil, span):
    rows = max(ATT_SLOTS // dil, 1)
    cols = min(dil, ATT_SLOTS)
    row_steps = span // (ATT_BLK * dil) // rows
    col_steps = dil // cols
    return rows, cols, row_steps, col_steps


def _attn_body(*refs, bw, dils, span):
    ng = len(dils)
    hm_ref, bias_ref, yb_ref, onat, mnat, lnat = refs[5 * ng:]
    blk = ATT_BLK
    nsl = bw // LANES
    j = pl.program_id(1)
    it = pl.program_id(2)
    hm = hm_ref[...]
    bias = bias_ref[...]
    for g, dil in enumerate(dils):
        q_ref, kc_ref, kp_ref, vc_ref, vp_ref = refs[5 * g:5 * g + 5]
        rows, cols, _, col_steps = _attn_tiling(dil, span)
        row_step = it // col_steps if col_steps > 1 else it
        col_step = it % col_steps if col_steps > 1 else 0
        for a in range(rows):
            for c in range(cols):
                cs = slice(c * bw, (c + 1) * bw)
                q = q_ref[0, a * blk:(a + 1) * blk, cs]
                if a == 0:
                    kk = jnp.concatenate([kp_ref[0, :, cs], kc_ref[0, 0:blk, cs]], axis=0)
                    vv = jnp.concatenate([vp_ref[0, :, cs], vc_ref[0, 0:blk, cs]], axis=0)
                    first = (j == 0) & (row_step == 0)
                else:
                    kk = kc_ref[0, (a - 1) * blk:(a + 1) * blk, cs]
                    vv = vc_ref[0, (a - 1) * blk:(a + 1) * blk, cs]
                    first = False
                stats = _attn_block(q, kk, vv, hm, bias, first, bw)
                start = (row_step * rows + a) * (blk * dil) + col_step * cols + c
                if dil == 1:
                    idx = pl.ds(pl.multiple_of(start, blk), blk)
                else:
                    idx = pl.ds(start, blk, stride=dil)
                for sl, (o_s, m_s, l_s) in enumerate(stats):
                    onat[g, sl, idx, :] = o_s
                    mnat[g, sl, idx, :] = m_s
                    lnat[g, sl, idx, :] = l_s

    @pl.when(it == pl.num_programs(2) - 1)
    def _():
        def merge(c, carry):
            rows = pl.ds(pl.multiple_of(c * MERGE_ROWS, MERGE_ROWS), MERGE_ROWS)
            for sl in range(nsl):
                ms = [mnat[g, sl, rows, :] for g in range(ng)]
                mm = functools.reduce(jnp.maximum, ms)
                ws = [jnp.exp2(m - mm) for m in ms]
                num = sum(w * onat[g, sl, rows, :] for g, w in enumerate(ws))
                den = sum(w * lnat[g, sl, rows, :] for g, w in enumerate(ws))
                yb_ref[0, rows, sl * LANES:(sl + 1) * LANES] = (num / den).astype(BF16)
            return carry
        lax.fori_loop(0, span // MERGE_ROWS, merge, 0)


def _attn_call(qkv, hm, bias, dils, bw):
    b, s, _ = qkv[0].shape
    blk = ATT_BLK
    span = blk * max(dils)
    nsteps = span // blk // ATT_SLOTS
    in_specs, args = [], []
    for g, dil in enumerate(dils):
        rows, cols, row_steps, col_steps = _attn_tiling(dil, span)
        assert row_steps * col_steps == nsteps

        def cur(i, j, it, rs=row_steps, cst=col_steps):
            return (i, j * rs + it // cst, it % cst)

        def prev(i, j, it, rs=row_steps, cst=col_steps, r=rows):
            return (i, jnp.maximum((j * rs + it // cst) * r - 1, 0), it % cst)

        q, k, v = qkv[3 * g:3 * g + 3]
        cur_spec = pl.BlockSpec((1, rows * blk, cols * bw), cur)
        prev_spec = pl.BlockSpec((1, blk, cols * bw), prev)
        in_specs += [cur_spec, cur_spec, prev_spec, cur_spec, prev_spec]
        args += [q, k, k, v, v]
    nsl = bw // LANES
    return pl.pallas_call(
        functools.partial(_attn_body, bw=bw, dils=dils, span=span),
        out_shape=jax.ShapeDtypeStruct((b, s, bw), BF16),
        grid=(b, s // span, nsteps),
        in_specs=in_specs + [_const_spec(hm.shape), _const_spec(bias.shape)],
        out_specs=pl.BlockSpec((1, span, bw), lambda i, j, u: (i, j, 0)),
        scratch_shapes=[pltpu.VMEM((len(dils), nsl, span, LANES), F32)] * 3,
        compiler_params=_params(3),
        name="band_attn",
    )(*args, hm, bias)


def _mix_body(x_ref, yb_ref,
              gmix_ref, win_ref, conva_ref, sgug_ref, sgub_ref, sguw_ref, sgubias_ref,
              confw_ref, confg_ref, confb_ref, wgate_ref, wbr_ref, wout_ref,
              xo_ref, exta, extd, *, ts, bw, kd):
    nslab = bw // LANES

    @pl.when(pl.program_id(1) == 0)
    def _():
        exta[:, 0:HALO_A, :] = jnp.zeros((nslab, HALO_A, LANES), F32)
        extd[:, 0:HALO_D, :] = jnp.zeros((nslab, HALO_D, LANES), F32)

    x = x_ref[0]
    hb = _rms(x, gmix_ref[...]).astype(BF16)
    a_b, a_c, a_x = (_dot(hb, win_ref[:, n * bw:(n + 1) * bw]) for n in range(3))
    s_u, s_v, c_val, c_gate = (_dot(hb, win_ref[:, n * bw:(n + 1) * bw]) for n in range(6, 10))
    gates = [_sigmoid(_dot(hb, wgate_ref[n])) for n in range(N_BRANCH)]

    ca = a_c * a_x
    ka = conva_ref.shape[0]
    ya = []
    for sl in range(nslab):
        ls = slice(sl * LANES, (sl + 1) * LANES)
        exta[sl, HALO_A:HALO_A + ts, :] = ca[:, ls]
        acc = None
        for t in range(ka):
            off = HALO_A - (ka - 1) + t
            term = conva_ref[t:t + 1, ls] * exta[sl, off:off + ts, :]
            acc = term if acc is None else acc + term
        ya.append(a_b[:, ls] * acc)
        exta[sl, 0:HALO_A, :] = exta[sl, ts:ts + HALO_A, :]
    y_a = jnp.concatenate(ya, axis=1)

    vb = _ln(s_v, sgug_ref[...], sgub_ref[...]).astype(BF16)
    ck = SGU_CHUNK
    wr = lax.broadcasted_iota(jnp.int32, (SGU_GROUPS * ck, ck), 0) & (ck - 1)
    wc = lax.broadcasted_iota(jnp.int32, (SGU_GROUPS * ck, ck), 1)
    wst = jnp.where(wc <= wr, sguw_ref[...], 0.0).astype(BF16)
    group_of_lane = lax.broadcasted_iota(jnp.int32, (ck, bw), 1) // (bw // SGU_GROUPS)
    yc = []
    for c in range(ts // ck):
        mm = _dot(wst, vb[c * ck:(c + 1) * ck, :])
        mixed = mm[0:ck]
        for g in range(1, SGU_GROUPS):
            mixed = jnp.where(group_of_lane == g, mm[g * ck:(g + 1) * ck], mixed)
        yc.append(s_u[c * ck:(c + 1) * ck, :] * (mixed + sgubias_ref[...]))
    y_c = jnp.concatenate(yc, axis=0)

    glu = c_val * _sigmoid(c_gate)
    for sl in range(nslab):
        extd[sl, HALO_D:HALO_D + ts, :] = glu[:, sl * LANES:(sl + 1) * LANES]
    cd = []
    n_pieces = nslab * (ts // CONV_ROWS)
    for sl in range(nslab):
        ls = slice(sl * LANES, (sl + 1) * LANES)
        pieces = []
        for r0 in range(0, ts, CONV_ROWS):
            acc = None
            for t in range(kd):
                off = r0 + HALO_D - (kd - 1) + t
                term = confw_ref[t:t + 1, ls] * extd[sl, off:off + CONV_ROWS, :]
                acc = term if acc is None else acc + term
            pi = len(pieces) + sl * (ts // CONV_ROWS)
            if pi % (n_pieces // N_BRANCH) == 0 and pi > 0:
                acc = _after(acc, gates[pi // (n_pieces // N_BRANCH) - 1])
            pieces.append(acc)
        cd.append(jnp.concatenate(pieces, axis=0))
        extd[sl, 0:HALO_D, :] = extd[sl, ts:ts + HALO_D, :]
    cd = _ln(jnp.concatenate(cd, axis=1), confg_ref[...], confb_ref[...])
    y_d = cd * _sigmoid(cd)

    merged = None
    for n, y in enumerate((y_a, yb_ref[0], y_c, y_d)):
        term = gates[n] * _dot(y.astype(BF16), wbr_ref[n])
        merged = term if merged is None else merged + term
    xo_ref[0] = x + _dot(merged.astype(BF16), wout_ref[...])


def _mix_call(x, y_b, layer, gmix, win, conva, sgug, sgub, sguw, sgubias,
              confw, confg, confb, wgate, wbr, wout):
    b, s, d = x.shape
    bw = conva.shape[-1]
    kd = confw.shape[1]
    ts = min(TS_MIX, s)
    nslab = bw // LANES
    tile = lambda w: pl.BlockSpec((1, ts, w), lambda i, j: (i, j, 0))
    consts = (gmix, win, conva, sgug, sgub, sguw, sgubias, confw, confg, confb,
              wgate, wbr, wout)
    return pl.pallas_call(
        functools.partial(_mix_body, ts=ts, bw=bw, kd=kd),
        out_shape=jax.ShapeDtypeStruct((b, s, d), F32),
        grid=(b, s // ts),
        in_specs=[tile(d), tile(bw)] + [_layer_spec(c, layer) for c in consts],
        out_specs=tile(d),
        scratch_shapes=[pltpu.VMEM((nslab, HALO_A + ts, LANES), F32),
                        pltpu.VMEM((nslab, HALO_D + ts, LANES), F32)],
        compiler_params=_params(2),
        name="mix_merge",
    )(x, y_b, *consts)


def _ffn_body(x_ref, p_ref, gffn_ref, w1_ref, w2_ref, gple_ref, wpg_ref, wpp_ref,
              gnext_ref, *rest, ts, fh, bw, dils, final):
    if final:
        xo_ref, hid_ref = rest
    else:
        n_out = 3 * len(dils)
        win_ref, xo_ref = rest[:2]
        qkv_refs, stages, hid_ref = rest[2:2 + n_out], rest[2 + n_out:-1], rest[-1]
    rs = ts // FFN_SUB
    subs = [slice(h * rs, (h + 1) * rs) for h in range(FFN_SUB)]
    xs = [x_ref[0, r, :] for r in subs]
    hbs = [_rms(x, gffn_ref[...]).astype(BF16) for x in xs]
    for c in range(fh // FFN_CHUNK):
        c0 = c * FFN_CHUNK
        for r, hb in zip(subs, hbs):
            gate = _dot(hb, w1_ref[:, c0:c0 + FFN_CHUNK])
            up = _dot(hb, w1_ref[:, fh + c0:fh + c0 + FFN_CHUNK])
            hid_ref[r, c0:c0 + FFN_CHUNK] = (gate * _sigmoid(gate) * up).astype(BF16)
    xs = [x + _dot(hid_ref[r, :], w2_ref[...]) for r, x in zip(subs, xs)]
    h3s = [_rms(x, gple_ref[...]).astype(BF16) for x in xs]
    xs = [x + _sigmoid(_dot(h3, wpg_ref[...])) * _dot(p_ref[0, r, :].astype(BF16), wpp_ref[...])
          for r, x, h3 in zip(subs, xs, h3s)]
    for r, x in zip(subs, xs):
        if final:
            xo_ref[0, r, :] = _rms(x, gnext_ref[...])
        else:
            xo_ref[0, r, :] = x
            _emit_qkv(x, gnext_ref, win_ref, qkv_refs, stages, row0=r.start, bw=bw, dils=dils)


def _ffn_call(x, p, layer, gffn, w1, w2, gple, wpg, wpp, gnext, win, dils, bw):
    b, s, d = x.shape
    fh = w2.shape[1]
    final = win is None
    ts = min(TS_FFN, s)
    tile = lambda w: pl.BlockSpec((1, ts, w), lambda i, j: (i, j, 0))
    in_specs = [tile(d),
                pl.BlockSpec((None, 1, ts, p.shape[-1]), lambda i, j: (layer, i, j, 0))]
    in_specs += [_layer_spec(c, layer) for c in (gffn, w1, w2, gple, wpg, wpp)]
    out_shape = [jax.ShapeDtypeStruct((b, s, d), F32)]
    out_specs = [tile(d)]
    scratch = [pltpu.VMEM((ts, fh), BF16)]
    if final:
        consts = (gnext,)
        in_specs.append(_layer_spec(gnext, 0))
    else:
        consts = (gnext, win)
        in_specs += [_layer_spec(gnext, layer + 1), _layer_spec(win, layer + 1)]
        qs, qspecs, qscratch = _qkv_outs(b, s, ts, bw, dils)
        out_shape += qs
        out_specs += qspecs
        scratch = qscratch + scratch
    outs = pl.pallas_call(
        functools.partial(_ffn_body, ts=ts, fh=fh, bw=bw, dils=dils, final=final),
        out_shape=tuple(out_shape),
        grid=(b, s // ts),
        in_specs=in_specs,
        out_specs=tuple(out_specs),
        scratch_shapes=scratch,
        compiler_params=_params(2),
        name="ffn_ple",
    )(x, p, gffn, w1, w2, gple, wpg, wpp, *consts)
    return outs[0], outs[1:]


def kernel(x, p, g_mix, w_in, conv_a, sgu_ln_g, sgu_ln_b, sgu_w, sgu_b, conf_dw,
           conf_ln_g, conf_ln_b, w_branch, w_merge_gate, w_out, g_ffn, w_ffn_in,
           w_ffn_out, g_ple, w_ple_gate, w_ple_proj, g_final):
    depth = w_in.shape[0]
    d = x.shape[-1]
    bw = d // N_BRANCH
    assert all(w // dil == ATT_BLK for w, dil in DSW_GROUPS)
    dils = tuple(dil for _, dil in DSW_GROUPS)
    blk = ATT_BLK
    head_of_col = jnp.arange(bw, dtype=jnp.int32) // (bw // ATT_HEADS)
    head_of_row = jnp.arange(ATT_HEADS * blk, dtype=jnp.int32) // blk
    hm = (head_of_row[:, None] == head_of_col[None, :]).astype(BF16)
    qi = (jnp.arange(ATT_HEADS * blk, dtype=jnp.int32) % blk)[:, None]
    ki = jnp.arange(2 * blk, dtype=jnp.int32)[None, :]
    bias = jnp.where((ki >= qi) & (ki <= qi + blk), 0.0, NEG).astype(F32)

    rows = lambda a: a.reshape(a.shape[0], 1, a.shape[-1])
    g_mix, sgu_ln_g, sgu_ln_b, conf_ln_g, conf_ln_b, g_ffn, g_ple = map(
        rows, (g_mix, sgu_ln_g, sgu_ln_b, conf_ln_g, conf_ln_b, g_ffn, g_ple))
    g_final = g_final.reshape(1, 1, d)
    sgu_w = sgu_w.reshape(depth, SGU_GROUPS * SGU_CHUNK, SGU_CHUNK)
    sgubias = jnp.repeat(jnp.swapaxes(sgu_b, 1, 2), bw // SGU_GROUPS, axis=2)
    w_in, w_merge_gate, w_branch, w_out, w_ffn_in, w_ffn_out, w_ple_gate, w_ple_proj = (
        w.astype(BF16) for w in (w_in, w_merge_gate, w_branch, w_out, w_ffn_in, w_ffn_out,
                                 w_ple_gate, w_ple_proj))

    qkv = _qkv_call(x, g_mix, w_in, 0, dils, bw)
    for i in range(depth):
        final = i == depth - 1
        y_b = _attn_call(qkv, hm, bias, dils, bw)
        x = _mix_call(x, y_b, i, g_mix, w_in, conv_a, sgu_ln_g, sgu_ln_b, sgu_w, sgubias,
                      conf_dw, conf_ln_g, conf_ln_b, w_merge_gate, w_branch, w_out)
        x, qkv = _ffn_call(x, p, i, g_ffn, w_ffn_in, w_ffn_out, g_ple, w_ple_gate, w_ple_proj,
                           g_final if final else g_mix, None if final else w_in, dils, bw)
    return x
```

```python
import functools

import jax
import jax.numpy as jnp
from jax import lax
from jax.experimental import pallas as pl
from jax.experimental.pallas import tpu as pltpu

F32 = jnp.float32
BF16 = jnp.bfloat16

EPS = 1e-6
N_BRANCH = 4
ATT_HEADS = 4
DSW_GROUPS = ((128, 1), (512, 4), (2048, 16))
ATT_BLK = 128
ATT_SLOTS = 8
SGU_CHUNK = 128
SGU_GROUPS = 4
LANES = 128
BF16_ROWS = 16
HALO_A = 8
HALO_D = 32
CONV_ROWS = 64
MERGE_ROWS = 256
FFN_CHUNK = 256
FFN_SUB = 2
QKV_SUB = 2
NEG = -float("inf")
LOG2E = 1.4426950408889634
VMEM_LIMIT = 56 * 1024 * 1024

TS_QKV = 1024
TS_MIX = 512
TS_FFN = 512


def _rms(x, g):
    ms = jnp.mean(x * x, axis=-1, keepdims=True)
    return x * lax.rsqrt(ms + EPS) * g


def _ln(x, g, b):
    mu = jnp.mean(x, axis=-1, keepdims=True)
    xc = x - mu
    var = jnp.mean(xc * xc, axis=-1, keepdims=True)
    return xc * lax.rsqrt(var + EPS) * g + b


def _sigmoid(x):
    return 1.0 / (1.0 + jnp.exp(-x))


def _dot(a, b):
    return jnp.dot(a, b, preferred_element_type=F32)


def _after(x, anchor):
    z = pltpu.bitcast(anchor[-8:, -LANES:], jnp.uint32)
    z = lax.shift_right_logical(lax.shift_right_logical(z, jnp.uint32(16)), jnp.uint32(16))
    xi = pltpu.bitcast(x, jnp.uint32) + jnp.tile(z, (x.shape[0] // 8, x.shape[1] // LANES))
    return pltpu.bitcast(xi, F32)


def _const_spec(shape):
    nd = len(shape)
    return pl.BlockSpec(shape, lambda *_: (0,) * nd, pipeline_mode=pl.Buffered(1))


def _layer_spec(arr, layer):
    nd = arr.ndim - 1
    return pl.BlockSpec((None,) + arr.shape[1:], lambda *_: (layer,) + (0,) * nd,
                        pipeline_mode=pl.Buffered(1))


def _cast_rows(r, steps):
    br = BF16_ROWS * pl.cdiv(pl.cdiv(r, steps), BF16_ROWS)
    while r % br:
        br += BF16_ROWS
    return br


def _params(n_grid):
    return pltpu.CompilerParams(
        dimension_semantics=("arbitrary",) * n_grid,
        vmem_limit_bytes=VMEM_LIMIT)


def _emit_qkv(x, g_ref, win_ref, outs, stages, *, row0, bw, dils):
    rs = x.shape[0]
    h = _rms(x, g_ref[...]).astype(BF16)
    qkv = _dot(h, win_ref[:, 3 * bw:6 * bw])
    nsl = bw // LANES
    scale = float((bw // ATT_HEADS) ** -0.5 * LOG2E)
    for n, dil in enumerate(dils):
        last = n == len(dils) - 1
        dst = slice(row0 // dil, (row0 + rs) // dil)
        for c in range(3 * nsl):
            t, sl = divmod(c, nsl)
            for r in range(dil):
                if n == 0:
                    rows = qkv[:, c * LANES:(c + 1) * LANES]
                    rows = rows * scale if t == 0 else rows
                else:
                    prev = dils[n - 1]
                    rows = stages[n - 1][c, r % prev,
                                         pl.ds(row0 // prev + r // prev, rs // dil,
                                               stride=dil // prev), :]
                if not last:
                    stages[n][c, r, dst, :] = rows
                lo = r * bw + sl * LANES
                outs[3 * n + t][0, dst, lo:lo + LANES] = rows.astype(BF16)


def _qkv_outs(b, s, ts, bw, dils):
    assert all(b_ % a_ == 0 for a_, b_ in zip(dils, dils[1:])) and dils[0] == 1
    out_shape, out_specs = [], []
    for dil in dils:
        for _ in range(3):
            out_shape.append(jax.ShapeDtypeStruct((b, s // dil, dil * bw), BF16))
            out_specs.append(pl.BlockSpec((1, ts // dil, dil * bw), lambda i, j: (i, j, 0)))
    scratch = [pltpu.VMEM((3 * bw // LANES, dil, ts // dil, LANES), F32) for dil in dils[:-1]]
    return out_shape, out_specs, scratch


def _qkv_body(x_ref, g_ref, win_ref, *rest, ts, bw, dils):
    n_out = 3 * len(dils)
    rs = ts // QKV_SUB
    for h in range(QKV_SUB):
        _emit_qkv(x_ref[0, h * rs:(h + 1) * rs, :], g_ref, win_ref, rest[:n_out], rest[n_out:],
                  row0=h * rs, bw=bw, dils=dils)


def _qkv_call(x, g, win, layer, dils, bw):
    b, s, d = x.shape
    ts = min(TS_QKV, s)
    out_shape, out_specs, scratch = _qkv_outs(b, s, ts, bw, dils)
    return pl.pallas_call(
        functools.partial(_qkv_body, ts=ts, bw=bw, dils=dils),
        out_shape=tuple(out_shape),
        grid=(b, s // ts),
        in_specs=[pl.BlockSpec((1, ts, d), lambda i, j: (i, j, 0)),
                  _layer_spec(g, layer), _layer_spec(win, 0)],
        out_specs=tuple(out_specs),
        scratch_shapes=scratch,
        compiler_params=_params(2),
        name="qkv_proj",
    )(x, g, win)


def _attn_block(q, kk, vv, hm, bias, first, bw):
    blk = ATT_BLK
    nh = ATT_HEADS
    dh = bw // nh
    hps = LANES // dh
    qst = jnp.concatenate([q] * nh, axis=0) * hm
    s = lax.dot_general(qst, kk, (((1,), (1,)), ((), ())), preferred_element_type=F32)
    s = s + bias
    if first is not False:
        s = jnp.concatenate([jnp.where(first, NEG, s[:, :blk]), s[:, blk:]], axis=1)
    m = jnp.max(s, axis=1, keepdims=True)
    p = jnp.exp2(s - m).astype(BF16)
    ones = jnp.ones((2 * blk, LANES), BF16)
    head_in_slab = lax.broadcasted_iota(jnp.int32, (blk, LANES), 1) // dh
    out = []
    for sl in range(bw // LANES):
        r0 = sl * hps * blk
        rhs = jnp.concatenate([vv[:, sl * LANES:(sl + 1) * LANES], ones], axis=1)
        ol = _dot(p[r0:r0 + hps * blk], rhs)
        o_s, l_s = ol[0:blk, :LANES], ol[0:blk, LANES:]
        m_s = jnp.broadcast_to(m[r0:r0 + blk], (blk, LANES))
        for hh in range(1, hps):
            sel = head_in_slab == hh
            rows = slice(hh * blk, (hh + 1) * blk)
            o_s = jnp.where(sel, ol[rows, :LANES], o_s)
            l_s = jnp.where(sel, ol[rows, LANES:], l_s)
            m_s = jnp.where(sel, m[r0 + hh * blk:r0 + (hh + 1) * blk], m_s)
        out.append((o_s, m_s, l_s))
    return out


def _attn_tiling(dil, span):
    rows = max(ATT_SLOTS // dil, 1)
    cols = min(dil, ATT_SLOTS)
    row_steps = span // (ATT_BLK * dil) // rows
    col_steps = dil // cols
    return rows, cols, row_steps, col_steps


def _attn_body(*refs, bw, dils, span):
    ng = len(dils)
    hm_ref, bias_ref, yb_ref, onat, mnat, lnat = refs[5 * ng:]
    blk = ATT_BLK
    nsl = bw // LANES
    j = pl.program_id(1)
    it = pl.program_id(2)
    hm = hm_ref[...]
    bias = bias_ref[...]
    for g, dil in enumerate(dils):
        q_ref, kc_ref, kp_ref, vc_ref, vp_ref = refs[5 * g:5 * g + 5]
        rows, cols, _, col_steps = _attn_tiling(dil, span)
        row_step = it // col_steps if col_steps > 1 else it
        col_step = it % col_steps if col_steps > 1 else 0
        for a in range(rows):
            for c in range(cols):
                cs = slice(c * bw, (c + 1) * bw)
                q = q_ref[0, a * blk:(a + 1) * blk, cs]
                if a == 0:
                    kk = jnp.concatenate([kp_ref[0, :, cs], kc_ref[0, 0:blk, cs]], axis=0)
                    vv = jnp.concatenate([vp_ref[0, :, cs], vc_ref[0, 0:blk, cs]], axis=0)
                    first = (j == 0) & (row_step == 0)
                else:
                    kk = kc_ref[0, (a - 1) * blk:(a + 1) * blk, cs]
                    vv = vc_ref[0, (a - 1) * blk:(a + 1) * blk, cs]
                    first = False
                stats = _attn_block(q, kk, vv, hm, bias, first, bw)
                start = (row_step * rows + a) * (blk * dil) + col_step * cols + c
                if dil == 1:
                    idx = pl.ds(pl.multiple_of(start, blk), blk)
                else:
                    idx = pl.ds(start, blk, stride=dil)
                for sl, (o_s, m_s, l_s) in enumerate(stats):
                    onat[g, sl, idx, :] = o_s
                    mnat[g, sl, idx, :] = m_s
                    lnat[g, sl, idx, :] = l_s

    @pl.when(it == pl.num_programs(2) - 1)
    def _():
        def merge(c, carry):
            rows = pl.ds(pl.multiple_of(c * MERGE_ROWS, MERGE_ROWS), MERGE_ROWS)
            for sl in range(nsl):
                ms = [mnat[g, sl, rows, :] for g in range(ng)]
                mm = functools.reduce(jnp.maximum, ms)
                ws = [jnp.exp2(m - mm) for m in ms]
                num = sum(w * onat[g, sl, rows, :] for g, w in enumerate(ws))
                den = sum(w * lnat[g, sl, rows, :] for g, w in enumerate(ws))
                yb_ref[0, rows, sl * LANES:(sl + 1) * LANES] = (num / den).astype(BF16)
            return carry
        lax.fori_loop(0, span // MERGE_ROWS, merge, 0)


def _attn_call(qkv, hm, bias, dils, bw):
    b, s, _ = qkv[0].shape
    blk = ATT_BLK
    span = blk * max(dils)
    nsteps = span // blk // ATT_SLOTS
    in_specs, args = [], []
    for g, dil in enumerate(dils):
        rows, cols, row_steps, col_steps = _attn_tiling(dil, span)
        assert row_steps * col_steps == nsteps

        def cur(i, j, it, rs=row_steps, cst=col_steps):
            return (i, j * rs + it // cst, it % cst)

        def prev(i, j, it, rs=row_steps, cst=col_steps, r=rows):
            return (i, jnp.maximum((j * rs + it // cst) * r - 1, 0), it % cst)

        q, k, v = qkv[3 * g:3 * g + 3]
        cur_spec = pl.BlockSpec((1, rows * blk, cols * bw), cur)
        prev_spec = pl.BlockSpec((1, blk, cols * bw), prev)
        in_specs += [cur_spec, cur_spec, prev_spec, cur_spec, prev_spec]
        args += [q, k, k, v, v]
    nsl = bw // LANES
    return pl.pallas_call(
        functools.partial(_attn_body, bw=bw, dils=dils, span=span),
        out_shape=jax.ShapeDtypeStruct((b, s, bw), BF16),
        grid=(b, s // span, nsteps),
        in_specs=in_specs + [_const_spec(hm.shape), _const_spec(bias.shape)],
        out_specs=pl.BlockSpec((1, span, bw), lambda i, j, u: (i, j, 0)),
        scratch_shapes=[pltpu.VMEM((len(dils), nsl, span, LANES), F32)] * 3,
        compiler_params=_params(3),
        name="band_attn",
    )(*args, hm, bias)


def _mix_body(x_ref, yb_ref,
              gmix_ref, win_ref, conva_ref, sgug_ref, sgub_ref, sguw_ref, sgubias_ref,
              confw_ref, confg_ref, confb_ref, wgate_ref, wbr_ref, wout_ref,
              *rest, ts, bw, kd, n_cast):
    cast_in, xo_ref, cast_out = rest[:n_cast], rest[n_cast], rest[n_cast + 1:2 * n_cast + 1]
    exta, extd = rest[2 * n_cast + 1:]
    nslab = bw // LANES
    for src, dst in zip(cast_in, cast_out):
        dst[...] = src[...].astype(BF16)

    @pl.when(pl.program_id(1) == 0)
    def _():
        exta[:, 0:HALO_A, :] = jnp.zeros((nslab, HALO_A, LANES), F32)
        extd[:, 0:HALO_D, :] = jnp.zeros((nslab, HALO_D, LANES), F32)

    x = x_ref[0]
    hb = _rms(x, gmix_ref[...]).astype(BF16)
    a_b, a_c, a_x = (_dot(hb, win_ref[:, n * bw:(n + 1) * bw]) for n in range(3))
    s_u, s_v, c_val, c_gate = (_dot(hb, win_ref[:, n * bw:(n + 1) * bw]) for n in range(6, 10))
    gates = [_sigmoid(_dot(hb, wgate_ref[n])) for n in range(N_BRANCH)]

    ca = a_c * a_x
    ka = conva_ref.shape[0]
    ya = []
    for sl in range(nslab):
        ls = slice(sl * LANES, (sl + 1) * LANES)
        exta[sl, HALO_A:HALO_A + ts, :] = ca[:, ls]
        acc = None
        for t in range(ka):
            off = HALO_A - (ka - 1) + t
            term = conva_ref[t:t + 1, ls] * exta[sl, off:off + ts, :]
            acc = term if acc is None else acc + term
        ya.append(a_b[:, ls] * acc)
        exta[sl, 0:HALO_A, :] = exta[sl, ts:ts + HALO_A, :]
    y_a = jnp.concatenate(ya, axis=1)

    vb = _ln(s_v, sgug_ref[...], sgub_ref[...]).astype(BF16)
    ck = SGU_CHUNK
    wr = lax.broadcasted_iota(jnp.int32, (SGU_GROUPS * ck, ck), 0) & (ck - 1)
    wc = lax.broadcasted_iota(jnp.int32, (SGU_GROUPS * ck, ck), 1)
    wst = jnp.where(wc <= wr, sguw_ref[...], 0.0).astype(BF16)
    group_of_lane = lax.broadcasted_iota(jnp.int32, (ck, bw), 1) // (bw // SGU_GROUPS)
    yc = []
    for c in range(ts // ck):
        mm = _dot(wst, vb[c * ck:(c + 1) * ck, :])
        mixed = mm[0:ck]
        for g in range(1, SGU_GROUPS):
            mixed = jnp.where(group_of_lane == g, mm[g * ck:(g + 1) * ck], mixed)
        yc.append(s_u[c * ck:(c + 1) * ck, :] * (mixed + sgubias_ref[...]))
    y_c = jnp.concatenate(yc, axis=0)

    glu = c_val * _sigmoid(c_gate)
    for sl in range(nslab):
        extd[sl, HALO_D:HALO_D + ts, :] = glu[:, sl * LANES:(sl + 1) * LANES]
    cd = []
    n_pieces = nslab * (ts // CONV_ROWS)
    for sl in range(nslab):
        ls = slice(sl * LANES, (sl + 1) * LANES)
        pieces = []
        for r0 in range(0, ts, CONV_ROWS):
            acc = None
            for t in range(kd):
                off = r0 + HALO_D - (kd - 1) + t
                term = confw_ref[t:t + 1, ls] * extd[sl, off:off + CONV_ROWS, :]
                acc = term if acc is None else acc + term
            pi = len(pieces) + sl * (ts // CONV_ROWS)
            if pi % (n_pieces // N_BRANCH) == 0 and pi > 0:
                acc = _after(acc, gates[pi // (n_pieces // N_BRANCH) - 1])
            pieces.append(acc)
        cd.append(jnp.concatenate(pieces, axis=0))
        extd[sl, 0:HALO_D, :] = extd[sl, ts:ts + HALO_D, :]
    cd = _ln(jnp.concatenate(cd, axis=1), confg_ref[...], confb_ref[...])
    y_d = cd * _sigmoid(cd)

    merged = None
    for n, y in enumerate((y_a, yb_ref[0], y_c, y_d)):
        term = gates[n] * _dot(y.astype(BF16), wbr_ref[n])
        merged = term if merged is None else merged + term
    xo_ref[0] = x + _dot(merged.astype(BF16), wout_ref[...])


def _mix_call(x, y_b, layer, gmix, win, conva, sgug, sgub, sguw, sgubias,
              confw, confg, confb, wgate, wbr, wout, next_f32):
    b, s, d = x.shape
    bw = conva.shape[-1]
    kd = confw.shape[1]
    ts = min(TS_MIX, s)
    n_seq = s // ts
    nslab = bw // LANES
    tile = lambda w: pl.BlockSpec((1, ts, w), lambda i, j: (i, j, 0))
    consts = (gmix, win, conva, sgug, sgub, sguw, sgubias, confw, confg, confb,
              wgate, wbr, wout)
    own = (win, wgate, wbr, wout)
    in_specs = [tile(d), tile(bw)]
    in_specs += [_layer_spec(c, 0 if any(c is o for o in own) else layer) for c in consts]
    out_shape = [jax.ShapeDtypeStruct((b, s, d), F32)]
    out_specs = [tile(d)]
    next_f32 = next_f32 or ()
    for w in next_f32:
        _, r, c = w.shape
        br = _cast_rows(r, b * n_seq)
        at = lambda i, j, last=r // br - 1: jnp.minimum(i * n_seq + j, last)
        in_specs.append(pl.BlockSpec((None, br, c), lambda i, j, at=at: (layer + 1, at(i, j), 0)))
        out_specs.append(pl.BlockSpec((br, c), lambda i, j, at=at: (at(i, j), 0)))
        out_shape.append(jax.ShapeDtypeStruct((r, c), BF16))
    outs = pl.pallas_call(
        functools.partial(_mix_body, ts=ts, bw=bw, kd=kd, n_cast=len(next_f32)),
        out_shape=tuple(out_shape),
        grid=(b, n_seq),
        in_specs=in_specs,
        out_specs=tuple(out_specs),
        scratch_shapes=[pltpu.VMEM((nslab, HALO_A + ts, LANES), F32),
                        pltpu.VMEM((nslab, HALO_D + ts, LANES), F32)],
        compiler_params=_params(2),
        name="mix_merge",
    )(x, y_b, *consts, *next_f32)
    return outs[0], outs[1:]


def _ffn_body(x_ref, p_ref, gffn_ref, w1_ref, w2_ref, gple_ref, wpg_ref, wpp_ref,
              gnext_ref, *rest, ts, fh, bw, dils, final):
    if final:
        xo_ref, hid_ref = rest
    else:
        n_out = 3 * len(dils)
        win_ref, xo_ref = rest[:2]
        qkv_refs, stages, hid_ref = rest[2:2 + n_out], rest[2 + n_out:-1], rest[-1]
    rs = ts // FFN_SUB
    subs = [slice(h * rs, (h + 1) * rs) for h in range(FFN_SUB)]
    xs = [x_ref[0, r, :] for r in subs]
    hbs = [_rms(x, gffn_ref[...]).astype(BF16) for x in xs]
    for c in range(fh // FFN_CHUNK):
        c0 = c * FFN_CHUNK
        for r, hb in zip(subs, hbs):
            gate = _dot(hb, w1_ref[:, c0:c0 + FFN_CHUNK])
            up = _dot(hb, w1_ref[:, fh + c0:fh + c0 + FFN_CHUNK])
            hid_ref[r, c0:c0 + FFN_CHUNK] = (gate * _sigmoid(gate) * up).astype(BF16)
    xs = [x + _dot(hid_ref[r, :], w2_ref[...]) for r, x in zip(subs, xs)]
    h3s = [_rms(x, gple_ref[...]).astype(BF16) for x in xs]
    xs = [x + _sigmoid(_dot(h3, wpg_ref[...])) * _dot(p_ref[0, r, :].astype(BF16), wpp_ref[...])
          for r, x, h3 in zip(subs, xs, h3s)]
    for r, x in zip(subs, xs):
        if final:
            xo_ref[0, r, :] = _rms(x, gnext_ref[...])
        else:
            xo_ref[0, r, :] = x
            _emit_qkv(x, gnext_ref, win_ref, qkv_refs, stages, row0=r.start, bw=bw, dils=dils)


def _ffn_call(x, p, layer, gffn, w1, w2, gple, wpg, wpp, gnext, win, dils, bw):
    b, s, d = x.shape
    fh = w2.shape[1]
    final = win is None
    ts = min(TS_FFN, s)
    tile = lambda w: pl.BlockSpec((1, ts, w), lambda i, j: (i, j, 0))
    in_specs = [tile(d),
                pl.BlockSpec((None, 1, ts, p.shape[-1]), lambda i, j: (layer, i, j, 0))]
    in_specs += [_layer_spec(c, lyr) for c, lyr in ((gffn, layer), (w1, 0), (w2, 0),
                                                     (gple, layer), (wpg, 0), (wpp, 0))]
    out_shape = [jax.ShapeDtypeStruct((b, s, d), F32)]
    out_specs = [tile(d)]
    scratch = [pltpu.VMEM((ts, fh), BF16)]
    if final:
        consts = (gnext,)
        in_specs.append(_layer_spec(gnext, 0))
    else:
        consts = (gnext, win)
        in_specs += [_layer_spec(gnext, layer + 1), _layer_spec(win, 0)]
        qs, qspecs, qscratch = _qkv_outs(b, s, ts, bw, dils)
        out_shape += qs
        out_specs += qspecs
        scratch = qscratch + scratch
    outs = pl.pallas_call(
        functools.partial(_ffn_body, ts=ts, fh=fh, bw=bw, dils=dils, final=final),
        out_shape=tuple(out_shape),
        grid=(b, s // ts),
        in_specs=in_specs,
        out_specs=tuple(out_specs),
        scratch_shapes=scratch,
        compiler_params=_params(2),
        name="ffn_ple",
    )(x, p, gffn, w1, w2, gple, wpg, wpp, *consts)
    return outs[0], outs[1:]


def kernel(x, p, g_mix, w_in, conv_a, sgu_ln_g, sgu_ln_b, sgu_w, sgu_b, conf_dw,
           conf_ln_g, conf_ln_b, w_branch, w_merge_gate, w_out, g_ffn, w_ffn_in,
           w_ffn_out, g_ple, w_ple_gate, w_ple_proj, g_final):
    depth = w_in.shape[0]
    d = x.shape[-1]
    bw = d // N_BRANCH
    assert all(w // dil == ATT_BLK for w, dil in DSW_GROUPS)
    dils = tuple(dil for _, dil in DSW_GROUPS)
    blk = ATT_BLK
    head_of_col = jnp.arange(bw, dtype=jnp.int32) // (bw // ATT_HEADS)
    head_of_row = jnp.arange(ATT_HEADS * blk, dtype=jnp.int32) // blk
    hm = (head_of_row[:, None] == head_of_col[None, :]).astype(BF16)
    qi = (jnp.arange(ATT_HEADS * blk, dtype=jnp.int32) % blk)[:, None]
    ki = jnp.arange(2 * blk, dtype=jnp.int32)[None, :]
    bias = jnp.where((ki >= qi) & (ki <= qi + blk), 0.0, NEG).astype(F32)

    rows = lambda a: a.reshape(a.shape[0], 1, a.shape[-1])
    g_mix, sgu_ln_g, sgu_ln_b, conf_ln_g, conf_ln_b, g_ffn, g_ple = map(
        rows, (g_mix, sgu_ln_g, sgu_ln_b, conf_ln_g, conf_ln_b, g_ffn, g_ple))
    g_final = g_final.reshape(1, 1, d)
    sgu_w = sgu_w.reshape(depth, SGU_GROUPS * SGU_CHUNK, SGU_CHUNK)
    sgubias = jnp.repeat(jnp.swapaxes(sgu_b, 1, 2), bw // SGU_GROUPS, axis=2)
    w_f32 = (w_in, w_merge_gate, w_branch, w_out, w_ffn_in, w_ffn_out, w_ple_gate, w_ple_proj)
    shapes = [(1,) + w.shape[1:] for w in w_f32]
    w_f32 = [w.reshape(depth, -1, w.shape[-1]) for w in w_f32]
    wb = [w[0:1].astype(BF16).reshape(sh) for w, sh in zip(w_f32, shapes)]

    qkv = _qkv_call(x, g_mix, wb[0], 0, dils, bw)
    for i in range(depth):
        final = i == depth - 1
        win, wgate, wbr, wout, w1, w2, wpg, wpp = wb
        y_b = _attn_call(qkv, hm, bias, dils, bw)
        x, nxt = _mix_call(x, y_b, i, g_mix, win, conv_a, sgu_ln_g, sgu_ln_b, sgu_w, sgubias,
                           conf_dw, conf_ln_g, conf_ln_b, wgate, wbr, wout,
                           None if final else w_f32)
        wb = [w.reshape(sh) for w, sh in zip(nxt, shapes)]
        x, qkv = _ffn_call(x, p, i, g_ffn, w1, w2, g_ple, wpg, wpp,
                           g_final if final else g_mix, None if final else wb[0], dils, bw)
    return x
```

```python
import functools

import jax
import jax.numpy as jnp
from jax import lax
from jax.experimental import pallas as pl
from jax.experimental.pallas import tpu as pltpu

F32 = jnp.float32
BF16 = jnp.bfloat16

EPS = 1e-6
N_BRANCH = 4
ATT_HEADS = 4
DSW_GROUPS = ((128, 1), (512, 4), (2048, 16))
ATT_BLK = 128
ATT_SLOTS = 8
SGU_CHUNK = 128
SGU_GROUPS = 4
LANES = 128
BF16_ROWS = 16
HALO_A = 8
HALO_D = 32
CONV_ROWS = 64
MERGE_ROWS = 256
FFN_CHUNK = 256
FFN_SUB = 2
QKV_SUB = 2
NEG = -float("inf")
LOG2E = 1.4426950408889634
VMEM_LIMIT = 56 * 1024 * 1024

TS_QKV = 1024
TS_MIX = 512
TS_FFN = 512


def _rms(x, g):
    ms = jnp.mean(x * x, axis=-1, keepdims=True)
    return x * lax.rsqrt(ms + EPS) * g


def _ln(x, g, b):
    mu = jnp.mean(x, axis=-1, keepdims=True)
    xc = x - mu
    var = jnp.mean(xc * xc, axis=-1, keepdims=True)
    return xc * lax.rsqrt(var + EPS) * g + b


def _sigmoid(x):
    return 1.0 / (1.0 + jnp.exp(-x))


def _dot(a, b):
    return jnp.dot(a, b, preferred_element_type=F32)


def _after(x, anchor):
    z = pltpu.bitcast(anchor[-8:, -LANES:], jnp.uint32)
    z = lax.shift_right_logical(lax.shift_right_logical(z, jnp.uint32(16)), jnp.uint32(16))
    never = jnp.tile(z, (x.shape[0] // 8, x.shape[1] // LANES)) != 0
    return jnp.where(never, jnp.zeros_like(x), x)


def _const_spec(shape):
    nd = len(shape)
    return pl.BlockSpec(shape, lambda *_: (0,) * nd, pipeline_mode=pl.Buffered(1))


def _layer_spec(arr, layer):
    nd = arr.ndim - 1
    return pl.BlockSpec((None,) + arr.shape[1:], lambda *_: (layer,) + (0,) * nd,
                        pipeline_mode=pl.Buffered(1))


def _cast_rows(r, steps):
    br = BF16_ROWS * pl.cdiv(pl.cdiv(r, steps), BF16_ROWS)
    while r % br:
        br += BF16_ROWS
    return br


def _params(n_grid):
    return pltpu.CompilerParams(
        dimension_semantics=("arbitrary",) * n_grid,
        vmem_limit_bytes=VMEM_LIMIT)


def _emit_qkv(x, g_ref, win_ref, outs, stages, *, row0, bw, dils):
    rs = x.shape[0]
    h = _rms(x, g_ref[...]).astype(BF16)
    qkv = _dot(h, win_ref[:, 3 * bw:6 * bw])
    nsl = bw // LANES
    scale = float((bw // ATT_HEADS) ** -0.5 * LOG2E)
    for n, dil in enumerate(dils):
        last = n == len(dils) - 1
        dst = slice(row0 // dil, (row0 + rs) // dil)
        for c in range(3 * nsl):
            t, sl = divmod(c, nsl)
            for r in range(dil):
                if n == 0:
                    rows = qkv[:, c * LANES:(c + 1) * LANES]
                    rows = rows * scale if t == 0 else rows
                else:
                    prev = dils[n - 1]
                    rows = stages[n - 1][c, r % prev,
                                         pl.ds(row0 // prev + r // prev, rs // dil,
                                               stride=dil // prev), :]
                if not last:
                    stages[n][c, r, dst, :] = rows
                lo = r * bw + sl * LANES
                outs[3 * n + t][0, dst, lo:lo + LANES] = rows.astype(BF16)


def _qkv_outs(b, s, ts, bw, dils):
    assert all(b_ % a_ == 0 for a_, b_ in zip(dils, dils[1:])) and dils[0] == 1
    out_shape, out_specs = [], []
    for dil in dils:
        for _ in range(3):
            out_shape.append(jax.ShapeDtypeStruct((b, s // dil, dil * bw), BF16))
            out_specs.append(pl.BlockSpec((1, ts // dil, dil * bw), lambda i, j: (i, j, 0)))
    scratch = [pltpu.VMEM((3 * bw // LANES, dil, ts // dil, LANES), F32) for dil in dils[:-1]]
    return out_shape, out_specs, scratch


def _qkv_body(x_ref, g_ref, win_ref, *rest, ts, bw, dils):
    n_out = 3 * len(dils)
    rs = ts // QKV_SUB
    for h in range(QKV_SUB):
        _emit_qkv(x_ref[0, h * rs:(h + 1) * rs, :], g_ref, win_ref, rest[:n_out], rest[n_out:],
                  row0=h * rs, bw=bw, dils=dils)


def _qkv_call(x, g, win, layer, dils, bw):
    b, s, d = x.shape
    ts = min(TS_QKV, s)
    out_shape, out_specs, scratch = _qkv_outs(b, s, ts, bw, dils)
    return pl.pallas_call(
        functools.partial(_qkv_body, ts=ts, bw=bw, dils=dils),
        out_shape=tuple(out_shape),
        grid=(b, s // ts),
        in_specs=[pl.BlockSpec((1, ts, d), lambda i, j: (i, j, 0)),
                  _layer_spec(g, layer), _layer_spec(win, 0)],
        out_specs=tuple(out_specs),
        scratch_shapes=scratch,
        compiler_params=_params(2),
        name="qkv_proj",
    )(x, g, win)


def _attn_block(q, kk, vv, hm, bias, first, bw):
    blk = ATT_BLK
    nh = ATT_HEADS
    dh = bw // nh
    hps = LANES // dh
    qst = jnp.concatenate([q] * nh, axis=0) * hm
    s = lax.dot_general(qst, kk, (((1,), (1,)), ((), ())), preferred_element_type=F32)
    s = s + bias
    if first is not False:
        s = jnp.concatenate([jnp.where(first, NEG, s[:, :blk]), s[:, blk:]], axis=1)
    m = jnp.max(s, axis=1, keepdims=True)
    p = jnp.exp2(s - m).astype(BF16)
    ones = jnp.ones((2 * blk, LANES), BF16)
    head_in_slab = lax.broadcasted_iota(jnp.int32, (blk, LANES), 1) // dh
    out = []
    for sl in range(bw // LANES):
        r0 = sl * hps * blk
        rhs = jnp.concatenate([vv[:, sl * LANES:(sl + 1) * LANES], ones], axis=1)
        ol = _dot(p[r0:r0 + hps * blk], rhs)
        o_s, l_s = ol[0:blk, :LANES], ol[0:blk, LANES:]
        m_s = jnp.broadcast_to(m[r0:r0 + blk], (blk, LANES))
        for hh in range(1, hps):
            sel = head_in_slab == hh
            rows = slice(hh * blk, (hh + 1) * blk)
            o_s = jnp.where(sel, ol[rows, :LANES], o_s)
            l_s = jnp.where(sel, ol[rows, LANES:], l_s)
            m_s = jnp.where(sel, m[r0 + hh * blk:r0 + (hh + 1) * blk], m_s)
        out.append((o_s, m_s, l_s))
    return out


def _attn_tiling(dil, span):
    rows = max(ATT_SLOTS // dil, 1)
    cols = min(dil, ATT_SLOTS)
    row_steps = span // (ATT_BLK * dil) // rows
    col_steps = dil // cols
    return rows, cols, row_steps, col_steps


def _attn_body(*refs, bw, dils, span, n_cast):
    ng = len(dils)
    hm_ref, bias_ref = refs[5 * ng:5 * ng + 2]
    cast_in = refs[5 * ng + 2:5 * ng + 2 + n_cast]
    yb_ref = refs[5 * ng + 2 + n_cast]
    cast_out = refs[5 * ng + 3 + n_cast:5 * ng + 3 + 2 * n_cast]
    onat, mnat, lnat = refs[5 * ng + 3 + 2 * n_cast:]
    for src, dst in zip(cast_in, cast_out):
        dst[...] = src[...].astype(BF16)
    blk = ATT_BLK
    nsl = bw // LANES
    j = pl.program_id(1)
    it = pl.program_id(2)
    hm = hm_ref[...]
    bias = bias_ref[...]
    for g, dil in enumerate(dils):
        q_ref, kc_ref, kp_ref, vc_ref, vp_ref = refs[5 * g:5 * g + 5]
        rows, cols, _, col_steps = _attn_tiling(dil, span)
        row_step = it // col_steps if col_steps > 1 else it
        col_step = it % col_steps if col_steps > 1 else 0
        for a in range(rows):
            for c in range(cols):
                cs = slice(c * bw, (c + 1) * bw)
                q = q_ref[0, a * blk:(a + 1) * blk, cs]
                if a == 0:
                    kk = jnp.concatenate([kp_ref[0, :, cs], kc_ref[0, 0:blk, cs]], axis=0)
                    vv = jnp.concatenate([vp_ref[0, :, cs], vc_ref[0, 0:blk, cs]], axis=0)
                    first = (j == 0) & (row_step == 0)
                else:
                    kk = kc_ref[0, (a - 1) * blk:(a + 1) * blk, cs]
                    vv = vc_ref[0, (a - 1) * blk:(a + 1) * blk, cs]
                    first = False
                stats = _attn_block(q, kk, vv, hm, bias, first, bw)
                start = (row_step * rows + a) * (blk * dil) + col_step * cols + c
                if dil == 1:
                    idx = pl.ds(pl.multiple_of(start, blk), blk)
                else:
                    idx = pl.ds(start, blk, stride=dil)
                for sl, (o_s, m_s, l_s) in enumerate(stats):
                    onat[g, sl, idx, :] = o_s
                    mnat[g, sl, idx, :] = m_s
                    lnat[g, sl, idx, :] = l_s

    @pl.when(it == pl.num_programs(2) - 1)
    def _():
        def merge(c, carry):
            rows = pl.ds(pl.multiple_of(c * MERGE_ROWS, MERGE_ROWS), MERGE_ROWS)
            for sl in range(nsl):
                ms = [mnat[g, sl, rows, :] for g in range(ng)]
                mm = functools.reduce(jnp.maximum, ms)
                ws = [jnp.exp2(m - mm) for m in ms]
                num = sum(w * onat[g, sl, rows, :] for g, w in enumerate(ws))
                den = sum(w * lnat[g, sl, rows, :] for g, w in enumerate(ws))
                yb_ref[0, rows, sl * LANES:(sl + 1) * LANES] = (num / den).astype(BF16)
            return carry
        lax.fori_loop(0, span // MERGE_ROWS, merge, 0)


def _attn_call(qkv, hm, bias, dils, bw, casts):
    b, s, _ = qkv[0].shape
    blk = ATT_BLK
    span = blk * max(dils)
    nsteps = span // blk // ATT_SLOTS
    in_specs, args = [], []
    for g, dil in enumerate(dils):
        rows, cols, row_steps, col_steps = _attn_tiling(dil, span)
        assert row_steps * col_steps == nsteps

        def cur(i, j, it, rs=row_steps, cst=col_steps):
            return (i, j * rs + it // cst, it % cst)

        def prev(i, j, it, rs=row_steps, cst=col_steps, r=rows):
            return (i, jnp.maximum((j * rs + it // cst) * r - 1, 0), it % cst)

        q, k, v = qkv[3 * g:3 * g + 3]
        cur_spec = pl.BlockSpec((1, rows * blk, cols * bw), cur)
        prev_spec = pl.BlockSpec((1, blk, cols * bw), prev)
        in_specs += [cur_spec, cur_spec, prev_spec, cur_spec, prev_spec]
        args += [q, k, k, v, v]
    nsl = bw // LANES
    n_tiles = s // span
    out_shape = [jax.ShapeDtypeStruct((b, s, bw), BF16)]
    out_specs = [pl.BlockSpec((1, span, bw), lambda i, j, u: (i, j, 0))]
    cast_specs = []
    for w, layer in casts:
        _, r, c = w.shape
        br = _cast_rows(r, b * n_tiles * nsteps)
        at = lambda i, j, u, last=r // br - 1: jnp.minimum((i * n_tiles + j) * nsteps + u, last)
        cast_specs.append(pl.BlockSpec((None, br, c),
                                       lambda i, j, u, at=at, lyr=layer: (lyr, at(i, j, u), 0)))
        out_specs.append(pl.BlockSpec((br, c), lambda i, j, u, at=at: (at(i, j, u), 0)))
        out_shape.append(jax.ShapeDtypeStruct((r, c), BF16))
    outs = pl.pallas_call(
        functools.partial(_attn_body, bw=bw, dils=dils, span=span, n_cast=len(casts)),
        out_shape=tuple(out_shape),
        grid=(b, n_tiles, nsteps),
        in_specs=in_specs + [_const_spec(hm.shape), _const_spec(bias.shape)] + cast_specs,
        out_specs=tuple(out_specs),
        scratch_shapes=[pltpu.VMEM((len(dils), nsl, span, LANES), F32)] * 3,
        compiler_params=_params(3),
        name="band_attn",
    )(*args, hm, bias, *[w for w, _ in casts])
    return outs[0], outs[1:]


def _mix_body(x_ref, yb_ref,
              gmix_ref, win_ref, conva_ref, sgug_ref, sgub_ref, sguw_ref, sgubias_ref,
              confw_ref, confg_ref, confb_ref, wgate_ref, wbr_ref, wout_ref,
              xo_ref, exta, extd, *, ts, bw, kd):
    nslab = bw // LANES

    @pl.when(pl.program_id(1) == 0)
    def _():
        exta[:, 0:HALO_A, :] = jnp.zeros((nslab, HALO_A, LANES), F32)
        extd[:, 0:HALO_D, :] = jnp.zeros((nslab, HALO_D, LANES), F32)

    x = x_ref[0]
    hb = _rms(x, gmix_ref[...]).astype(BF16)
    a_b, a_c, a_x = (_dot(hb, win_ref[:, n * bw:(n + 1) * bw]) for n in range(3))
    s_u, s_v, c_val, c_gate = (_dot(hb, win_ref[:, n * bw:(n + 1) * bw]) for n in range(6, 10))
    gates = [_sigmoid(_dot(hb, wgate_ref[n])) for n in range(N_BRANCH)]

    ca = a_c * a_x
    ka = conva_ref.shape[0]
    ya = []
    for sl in range(nslab):
        ls = slice(sl * LANES, (sl + 1) * LANES)
        exta[sl, HALO_A:HALO_A + ts, :] = ca[:, ls]
        acc = None
        for t in range(ka):
            off = HALO_A - (ka - 1) + t
            term = conva_ref[t:t + 1, ls] * exta[sl, off:off + ts, :]
            acc = term if acc is None else acc + term
        ya.append(a_b[:, ls] * acc)
        exta[sl, 0:HALO_A, :] = exta[sl, ts:ts + HALO_A, :]
    y_a = jnp.concatenate(ya, axis=1)

    vb = _ln(s_v, sgug_ref[...], sgub_ref[...]).astype(BF16)
    ck = SGU_CHUNK
    wr = lax.broadcasted_iota(jnp.int32, (SGU_GROUPS * ck, ck), 0) & (ck - 1)
    wc = lax.broadcasted_iota(jnp.int32, (SGU_GROUPS * ck, ck), 1)
    wst = jnp.where(wc <= wr, sguw_ref[...], 0.0).astype(BF16)
    group_of_lane = lax.broadcasted_iota(jnp.int32, (ck, bw), 1) // (bw // SGU_GROUPS)
    yc = []
    for c in range(ts // ck):
        mm = _dot(wst, vb[c * ck:(c + 1) * ck, :])
        mixed = mm[0:ck]
        for g in range(1, SGU_GROUPS):
            mixed = jnp.where(group_of_lane == g, mm[g * ck:(g + 1) * ck], mixed)
        yc.append(s_u[c * ck:(c + 1) * ck, :] * (mixed + sgubias_ref[...]))
    y_c = jnp.concatenate(yc, axis=0)

    glu = c_val * _sigmoid(c_gate)
    for sl in range(nslab):
        extd[sl, HALO_D:HALO_D + ts, :] = glu[:, sl * LANES:(sl + 1) * LANES]
    cd = []
    n_pieces = nslab * (ts // CONV_ROWS)
    for sl in range(nslab):
        ls = slice(sl * LANES, (sl + 1) * LANES)
        pieces = []
        for r0 in range(0, ts, CONV_ROWS):
            acc = None
            for t in range(kd):
                off = r0 + HALO_D - (kd - 1) + t
                term = confw_ref[t:t + 1, ls] * extd[sl, off:off + CONV_ROWS, :]
                acc = term if acc is None else acc + term
            pi = len(pieces) + sl * (ts // CONV_ROWS)
            if pi % (n_pieces // N_BRANCH) == 0 and pi > 0:
                acc = _after(acc, gates[pi // (n_pieces // N_BRANCH) - 1])
            pieces.append(acc)
        cd.append(jnp.concatenate(pieces, axis=0))
        extd[sl, 0:HALO_D, :] = extd[sl, ts:ts + HALO_D, :]
    cd = _ln(jnp.concatenate(cd, axis=1), confg_ref[...], confb_ref[...])
    y_d = cd * _sigmoid(cd)

    merged = None
    for n, y in enumerate((y_a, yb_ref[0], y_c, y_d)):
        term = gates[n] * _dot(y.astype(BF16), wbr_ref[n])
        merged = term if merged is None else merged + term
    xo_ref[0] = x + _dot(merged.astype(BF16), wout_ref[...])


def _mix_call(x, y_b, layer, gmix, win, conva, sgug, sgub, sguw, sgubias,
              confw, confg, confb, wgate, wbr, wout):
    b, s, d = x.shape
    bw = conva.shape[-1]
    kd = confw.shape[1]
    ts = min(TS_MIX, s)
    nslab = bw // LANES
    tile = lambda w: pl.BlockSpec((1, ts, w), lambda i, j: (i, j, 0))
    consts = (gmix, win, conva, sgug, sgub, sguw, sgubias, confw, confg, confb,
              wgate, wbr, wout)
    own = (win, wgate, wbr, wout)
    return pl.pallas_call(
        functools.partial(_mix_body, ts=ts, bw=bw, kd=kd),
        out_shape=jax.ShapeDtypeStruct((b, s, d), F32),
        grid=(b, s // ts),
        in_specs=[tile(d), tile(bw)] + [_layer_spec(c, 0 if any(c is o for o in own) else layer) for c in consts],
        out_specs=tile(d),
        scratch_shapes=[pltpu.VMEM((nslab, HALO_A + ts, LANES), F32),
                        pltpu.VMEM((nslab, HALO_D + ts, LANES), F32)],
        compiler_params=_params(2),
        name="mix_merge",
    )(x, y_b, *consts)


def _ffn_body(x_ref, p_ref, gffn_ref, w1_ref, w2_ref, gple_ref, wpg_ref, wpp_ref,
              gnext_ref, *rest, ts, fh, bw, dils, final):
    if final:
        xo_ref, hid_ref = rest
    else:
        n_out = 3 * len(dils)
        win_ref, xo_ref = rest[:2]
        qkv_refs, stages, hid_ref = rest[2:2 + n_out], rest[2 + n_out:-1], rest[-1]
    rs = ts // FFN_SUB
    subs = [slice(h * rs, (h + 1) * rs) for h in range(FFN_SUB)]
    xs = [x_ref[0, r, :] for r in subs]
    hbs = [_rms(x, gffn_ref[...]).astype(BF16) for x in xs]
    for c in range(fh // FFN_CHUNK):
        c0 = c * FFN_CHUNK
        for r, hb in zip(subs, hbs):
            gate = _dot(hb, w1_ref[:, c0:c0 + FFN_CHUNK])
            up = _dot(hb, w1_ref[:, fh + c0:fh + c0 + FFN_CHUNK])
            hid_ref[r, c0:c0 + FFN_CHUNK] = (gate * _sigmoid(gate) * up).astype(BF16)
    xs = [x + _dot(hid_ref[r, :], w2_ref[...]) for r, x in zip(subs, xs)]
    h3s = [_rms(x, gple_ref[...]).astype(BF16) for x in xs]
    xs = [x + _sigmoid(_dot(h3, wpg_ref[...])) * _dot(p_ref[0, r, :].astype(BF16), wpp_ref[...])
          for r, x, h3 in zip(subs, xs, h3s)]
    for r, x in zip(subs, xs):
        if final:
            xo_ref[0, r, :] = _rms(x, gnext_ref[...])
        else:
            xo_ref[0, r, :] = x
            _emit_qkv(x, gnext_ref, win_ref, qkv_refs, stages, row0=r.start, bw=bw, dils=dils)


def _ffn_call(x, p, layer, gffn, w1, w2, gple, wpg, wpp, gnext, win, dils, bw):
    b, s, d = x.shape
    fh = w2.shape[1]
    final = win is None
    ts = min(TS_FFN, s)
    tile = lambda w: pl.BlockSpec((1, ts, w), lambda i, j: (i, j, 0))
    in_specs = [tile(d),
                pl.BlockSpec((None, 1, ts, p.shape[-1]), lambda i, j: (layer, i, j, 0))]
    in_specs += [_layer_spec(c, lyr) for c, lyr in ((gffn, layer), (w1, 0), (w2, 0),
                                                     (gple, layer), (wpg, 0), (wpp, 0))]
    out_shape = [jax.ShapeDtypeStruct((b, s, d), F32)]
    out_specs = [tile(d)]
    scratch = [pltpu.VMEM((ts, fh), BF16)]
    if final:
        consts = (gnext,)
        in_specs.append(_layer_spec(gnext, 0))
    else:
        consts = (gnext, win)
        in_specs += [_layer_spec(gnext, layer + 1), _layer_spec(win, 0)]
        qs, qspecs, qscratch = _qkv_outs(b, s, ts, bw, dils)
        out_shape += qs
        out_specs += qspecs
        scratch = qscratch + scratch
    outs = pl.pallas_call(
        functools.partial(_ffn_body, ts=ts, fh=fh, bw=bw, dils=dils, final=final),
        out_shape=tuple(out_shape),
        grid=(b, s // ts),
        in_specs=in_specs,
        out_specs=tuple(out_specs),
        scratch_shapes=scratch,
        compiler_params=_params(2),
        name="ffn_ple",
    )(x, p, gffn, w1, w2, gple, wpg, wpp, *consts)
    return outs[0], outs[1:]


def kernel(x, p, g_mix, w_in, conv_a, sgu_ln_g, sgu_ln_b, sgu_w, sgu_b, conf_dw,
           conf_ln_g, conf_ln_b, w_branch, w_merge_gate, w_out, g_ffn, w_ffn_in,
           w_ffn_out, g_ple, w_ple_gate, w_ple_proj, g_final):
    depth = w_in.shape[0]
    d = x.shape[-1]
    bw = d // N_BRANCH
    assert all(w // dil == ATT_BLK for w, dil in DSW_GROUPS)
    dils = tuple(dil for _, dil in DSW_GROUPS)
    blk = ATT_BLK
    head_of_col = jnp.arange(bw, dtype=jnp.int32) // (bw // ATT_HEADS)
    head_of_row = jnp.arange(ATT_HEADS * blk, dtype=jnp.int32) // blk
    hm = (head_of_row[:, None] == head_of_col[None, :]).astype(BF16)
    qi = (jnp.arange(ATT_HEADS * blk, dtype=jnp.int32) % blk)[:, None]
    ki = jnp.arange(2 * blk, dtype=jnp.int32)[None, :]
    bias = jnp.where((ki >= qi) & (ki <= qi + blk), 0.0, NEG).astype(F32)

    rows = lambda a: a.reshape(a.shape[0], 1, a.shape[-1])
    g_mix, sgu_ln_g, sgu_ln_b, conf_ln_g, conf_ln_b, g_ffn, g_ple = map(
        rows, (g_mix, sgu_ln_g, sgu_ln_b, conf_ln_g, conf_ln_b, g_ffn, g_ple))
    g_final = g_final.reshape(1, 1, d)
    sgu_w = sgu_w.reshape(depth, SGU_GROUPS * SGU_CHUNK, SGU_CHUNK)
    sgubias = jnp.repeat(jnp.swapaxes(sgu_b, 1, 2), bw // SGU_GROUPS, axis=2)
    w_f32 = (w_in, w_merge_gate, w_branch, w_out, w_ffn_in, w_ffn_out, w_ple_gate, w_ple_proj)
    shapes = [(1,) + w.shape[1:] for w in w_f32]
    w_f32 = [w.reshape(depth, -1, w.shape[-1]) for w in w_f32]
    win = w_f32[0][0:1].astype(BF16)

    qkv = _qkv_call(x, g_mix, win, 0, dils, bw)
    for i in range(depth):
        final = i == depth - 1
        casts = [(w, i) for w in w_f32[1:]] + ([] if final else [(w_f32[0], i + 1)])
        y_b, cast = _attn_call(qkv, hm, bias, dils, bw, casts)
        wgate, wbr, wout, w1, w2, wpg, wpp = (
            w.reshape(sh) for w, sh in zip(cast, shapes[1:]))
        x = _mix_call(x, y_b, i, g_mix, win, conv_a, sgu_ln_g, sgu_ln_b, sgu_w, sgubias,
                      conf_dw, conf_ln_g, conf_ln_b, wgate, wbr, wout)
        win = None if final else cast[-1].reshape(shapes[0])
        x, qkv = _ffn_call(x, p, i, g_ffn, w1, w2, g_ple, wpg, wpp,
                           g_final if final else g_mix, win, dils, bw)
    return x
```

```python
import functools

import jax
import jax.numpy as jnp
from jax import lax
from jax.experimental import pallas as pl
from jax.experimental.pallas import tpu as pltpu

F32 = jnp.float32
BF16 = jnp.bfloat16

EPS = 1e-6
N_BRANCH = 4
ATT_HEADS = 4
DSW_GROUPS = ((128, 1), (512, 4), (2048, 16))
ATT_BLK = 128
ATT_SLOTS = 8
SGU_CHUNK = 128
SGU_GROUPS = 4
LANES = 128
SUBLANES = 8
BF16_ROWS = 16
HALO_A = 8
HALO_D = 32
CONV_ROWS = 64
MERGE_ROWS = 256
FFN_CHUNK = 256
FFN_SUB = 2
QKV_SUB = 2
NEG = -float("inf")
LOG2E = 1.4426950408889634
VMEM_LIMIT = 56 * 1024 * 1024

TS_QKV = 1024
TS_MIX = 512
TS_FFN = 512


def _rms(x, g):
    ms = jnp.mean(x * x, axis=-1, keepdims=True)
    return x * lax.rsqrt(ms + EPS) * g


def _ln(x, g, b):
    mu = jnp.mean(x, axis=-1, keepdims=True)
    xc = x - mu
    var = jnp.mean(xc * xc, axis=-1, keepdims=True)
    return xc * lax.rsqrt(var + EPS) * g + b


def _sigmoid(x):
    return 1.0 / (1.0 + jnp.exp(-x))


def _dot(a, b):
    return jnp.dot(a, b, preferred_element_type=F32)


def _after(x, anchor):
    z = pltpu.bitcast(anchor[-SUBLANES:, -LANES:], jnp.uint32)
    z = lax.shift_right_logical(lax.shift_right_logical(z, jnp.uint32(16)), jnp.uint32(16))
    never = jnp.tile(z, (x.shape[0] // SUBLANES, x.shape[1] // LANES)) != 0
    return jnp.where(never, jnp.zeros_like(x), x)


def _const_spec(shape):
    nd = len(shape)
    return pl.BlockSpec(shape, lambda *_: (0,) * nd, pipeline_mode=pl.Buffered(1))


def _layer_spec(arr, layer):
    nd = arr.ndim - 1
    return pl.BlockSpec((None,) + arr.shape[1:], lambda *_: (layer,) + (0,) * nd,
                        pipeline_mode=pl.Buffered(1))


def _cast_rows(r, steps):
    br = BF16_ROWS * pl.cdiv(pl.cdiv(r, steps), BF16_ROWS)
    while r % br:
        br += BF16_ROWS
    return br


def _params(n_grid):
    return pltpu.CompilerParams(
        dimension_semantics=("arbitrary",) * n_grid,
        vmem_limit_bytes=VMEM_LIMIT)


def _emit_qkv(x, g_ref, win_ref, outs, stages, *, row0, bw, dils):
    rs = x.shape[0]
    h = _rms(x, g_ref[...]).astype(BF16)
    qkv = _dot(h, win_ref[:, 3 * bw:6 * bw])
    nsl = bw // LANES
    scale = float((bw // ATT_HEADS) ** -0.5 * LOG2E)
    for n, dil in enumerate(dils):
        last = n == len(dils) - 1
        dst = slice(row0 // dil, (row0 + rs) // dil)
        for c in range(3 * nsl):
            t, sl = divmod(c, nsl)
            for r in range(dil):
                if n == 0:
                    rows = qkv[:, c * LANES:(c + 1) * LANES]
                    rows = rows * scale if t == 0 else rows
                else:
                    prev = dils[n - 1]
                    rows = stages[n - 1][c, r % prev,
                                         pl.ds(row0 // prev + r // prev, rs // dil,
                                               stride=dil // prev), :]
                if not last:
                    stages[n][c, r, dst, :] = rows
                lo = r * bw + sl * LANES
                outs[3 * n + t][0, dst, lo:lo + LANES] = rows.astype(BF16)


def _qkv_outs(b, s, ts, bw, dils):
    assert all(b_ % a_ == 0 for a_, b_ in zip(dils, dils[1:])) and dils[0] == 1
    out_shape, out_specs = [], []
    for dil in dils:
        for _ in range(3):
            out_shape.append(jax.ShapeDtypeStruct((b, s // dil, dil * bw), BF16))
            out_specs.append(pl.BlockSpec((1, ts // dil, dil * bw), lambda i, j: (i, j, 0)))
    scratch = [pltpu.VMEM((3 * bw // LANES, dil, ts // dil, LANES), F32) for dil in dils[:-1]]
    return out_shape, out_specs, scratch


def _qkv_body(x_ref, g_ref, win_ref, *rest, ts, bw, dils):
    n_out = 3 * len(dils)
    rs = ts // QKV_SUB
    for h in range(QKV_SUB):
        _emit_qkv(x_ref[0, h * rs:(h + 1) * rs, :], g_ref, win_ref, rest[:n_out], rest[n_out:],
                  row0=h * rs, bw=bw, dils=dils)


def _qkv_call(x, g, win, layer, dils, bw):
    b, s, d = x.shape
    ts = min(TS_QKV, s)
    out_shape, out_specs, scratch = _qkv_outs(b, s, ts, bw, dils)
    return pl.pallas_call(
        functools.partial(_qkv_body, ts=ts, bw=bw, dils=dils),
        out_shape=tuple(out_shape),
        grid=(b, s // ts),
        in_specs=[pl.BlockSpec((1, ts, d), lambda i, j: (i, j, 0)),
                  _layer_spec(g, layer), _layer_spec(win, 0)],
        out_specs=tuple(out_specs),
        scratch_shapes=scratch,
        compiler_params=_params(2),
        name="qkv_proj",
    )(x, g, win)


def _attn_block(q, kk, vv, hm, bias, first, bw):
    blk = ATT_BLK
    nh = ATT_HEADS
    dh = bw // nh
    hps = LANES // dh
    qst = jnp.concatenate([q] * nh, axis=0) * hm
    s = lax.dot_general(qst, kk, (((1,), (1,)), ((), ())), preferred_element_type=F32)
    s = s + bias
    if first is not False:
        s = jnp.concatenate([jnp.where(first, NEG, s[:, :blk]), s[:, blk:]], axis=1)
    m = jnp.max(s, axis=1, keepdims=True)
    p = jnp.exp2(s - m).astype(BF16)
    ones = jnp.ones((2 * blk, LANES), BF16)
    head_in_slab = lax.broadcasted_iota(jnp.int32, (blk, LANES), 1) // dh
    out = []
    for sl in range(bw // LANES):
        r0 = sl * hps * blk
        rhs = jnp.concatenate([vv[:, sl * LANES:(sl + 1) * LANES], ones], axis=1)
        ol = _dot(p[r0:r0 + hps * blk], rhs)
        o_s, l_s = ol[0:blk, :LANES], ol[0:blk, LANES:]
        m_s = jnp.broadcast_to(m[r0:r0 + blk], (blk, LANES))
        for hh in range(1, hps):
            sel = head_in_slab == hh
            rows = slice(hh * blk, (hh + 1) * blk)
            o_s = jnp.where(sel, ol[rows, :LANES], o_s)
            l_s = jnp.where(sel, ol[rows, LANES:], l_s)
            m_s = jnp.where(sel, m[r0 + hh * blk:r0 + (hh + 1) * blk], m_s)
        out.append((o_s, m_s, l_s))
    return out


def _attn_tiling(dil, span):
    rows = max(ATT_SLOTS // dil, 1)
    cols = min(dil, ATT_SLOTS)
    row_steps = span // (ATT_BLK * dil) // rows
    col_steps = dil // cols
    return rows, cols, row_steps, col_steps


def _attn_body(*refs, bw, dils, span, n_cast):
    ng = len(dils)
    hm_ref, bias_ref = refs[5 * ng:5 * ng + 2]
    cast_in = refs[5 * ng + 2:5 * ng + 2 + n_cast]
    yb_ref = refs[5 * ng + 2 + n_cast]
    cast_out = refs[5 * ng + 3 + n_cast:5 * ng + 3 + 2 * n_cast]
    onat, mnat, lnat = refs[5 * ng + 3 + 2 * n_cast:]
    for src, dst in zip(cast_in, cast_out):
        dst[...] = src[...].astype(BF16)
    blk = ATT_BLK
    nsl = bw // LANES
    j = pl.program_id(1)
    it = pl.program_id(2)
    hm = hm_ref[...]
    bias = bias_ref[...]
    for g, dil in enumerate(dils):
        q_ref, kc_ref, kp_ref, vc_ref, vp_ref = refs[5 * g:5 * g + 5]
        rows, cols, _, col_steps = _attn_tiling(dil, span)
        row_step = it // col_steps if col_steps > 1 else it
        col_step = it % col_steps if col_steps > 1 else 0
        for a in range(rows):
            for c in range(cols):
                cs = slice(c * bw, (c + 1) * bw)
                q = q_ref[0, a * blk:(a + 1) * blk, cs]
                if a == 0:
                    kk = jnp.concatenate([kp_ref[0, :, cs], kc_ref[0, 0:blk, cs]], axis=0)
                    vv = jnp.concatenate([vp_ref[0, :, cs], vc_ref[0, 0:blk, cs]], axis=0)
                    first = (j == 0) & (row_step == 0)
                else:
                    kk = kc_ref[0, (a - 1) * blk:(a + 1) * blk, cs]
                    vv = vc_ref[0, (a - 1) * blk:(a + 1) * blk, cs]
                    first = False
                stats = _attn_block(q, kk, vv, hm, bias, first, bw)
                start = (row_step * rows + a) * (blk * dil) + col_step * cols + c
                if dil == 1:
                    idx = pl.ds(pl.multiple_of(start, blk), blk)
                else:
                    idx = pl.ds(start, blk, stride=dil)
                for sl, (o_s, m_s, l_s) in enumerate(stats):
                    onat[g, sl, idx, :] = o_s
                    mnat[g, sl, idx, :] = m_s
                    lnat[g, sl, idx, :] = l_s

    @pl.when(it == pl.num_programs(2) - 1)
    def _():
        def merge(c, carry):
            rows = pl.ds(pl.multiple_of(c * MERGE_ROWS, MERGE_ROWS), MERGE_ROWS)
            for sl in range(nsl):
                ms = [mnat[g, sl, rows, :] for g in range(ng)]
                mm = functools.reduce(jnp.maximum, ms)
                ws = [jnp.exp2(m - mm) for m in ms]
                num = sum(w * onat[g, sl, rows, :] for g, w in enumerate(ws))
                den = sum(w * lnat[g, sl, rows, :] for g, w in enumerate(ws))
                yb_ref[0, rows, sl * LANES:(sl + 1) * LANES] = (num / den).astype(BF16)
            return carry
        lax.fori_loop(0, span // MERGE_ROWS, merge, 0)


def _attn_call(qkv, hm, bias, dils, bw, casts):
    b, s, _ = qkv[0].shape
    blk = ATT_BLK
    span = blk * max(dils)
    nsteps = span // blk // ATT_SLOTS
    in_specs, args = [], []
    for g, dil in enumerate(dils):
        rows, cols, row_steps, col_steps = _attn_tiling(dil, span)
        assert row_steps * col_steps == nsteps

        def cur(i, j, it, rs=row_steps, cst=col_steps):
            return (i, j * rs + it // cst, it % cst)

        def prev(i, j, it, rs=row_steps, cst=col_steps, r=rows):
            return (i, jnp.maximum((j * rs + it // cst) * r - 1, 0), it % cst)

        q, k, v = qkv[3 * g:3 * g + 3]
        cur_spec = pl.BlockSpec((1, rows * blk, cols * bw), cur)
        prev_spec = pl.BlockSpec((1, blk, cols * bw), prev)
        in_specs += [cur_spec, cur_spec, prev_spec, cur_spec, prev_spec]
        args += [q, k, k, v, v]
    nsl = bw // LANES
    n_tiles = s // span
    out_shape = [jax.ShapeDtypeStruct((b, s, bw), BF16)]
    out_specs = [pl.BlockSpec((1, span, bw), lambda i, j, u: (i, j, 0))]
    cast_specs = []
    for w, layer in casts:
        _, r, c = w.shape
        br = _cast_rows(r, b * n_tiles * nsteps)
        at = lambda i, j, u, last=r // br - 1: jnp.minimum((i * n_tiles + j) * nsteps + u, last)
        cast_specs.append(pl.BlockSpec((None, br, c),
                                       lambda i, j, u, at=at, lyr=layer: (lyr, at(i, j, u), 0)))
        out_specs.append(pl.BlockSpec((br, c), lambda i, j, u, at=at: (at(i, j, u), 0)))
        out_shape.append(jax.ShapeDtypeStruct((r, c), BF16))
    outs = pl.pallas_call(
        functools.partial(_attn_body, bw=bw, dils=dils, span=span, n_cast=len(casts)),
        out_shape=tuple(out_shape),
        grid=(b, n_tiles, nsteps),
        in_specs=in_specs + [_const_spec(hm.shape), _const_spec(bias.shape)] + cast_specs,
        out_specs=tuple(out_specs),
        scratch_shapes=[pltpu.VMEM((len(dils), nsl, span, LANES), F32)] * 3,
        compiler_params=_params(3),
        name="band_attn",
    )(*args, hm, bias, *[w for w, _ in casts])
    return outs[0], outs[1:]


def _mix_body(x_ref, yb_ref,
              gmix_ref, win_ref, conva_ref, sgug_ref, sgub_ref, sguw_ref, sgubias_ref,
              confw_ref, confg_ref, confb_ref, wgate_ref, wbr_ref, wout_ref,
              xo_ref, exta, extd, *, ts, bw, kd):
    nslab = bw // LANES

    @pl.when(pl.program_id(1) == 0)
    def _():
        exta[:, 0:HALO_A, :] = jnp.zeros((nslab, HALO_A, LANES), F32)
        extd[:, 0:HALO_D, :] = jnp.zeros((nslab, HALO_D, LANES), F32)

    x = x_ref[0]
    hb = _rms(x, gmix_ref[...]).astype(BF16)
    a_b, a_c, a_x = (_dot(hb, win_ref[:, n * bw:(n + 1) * bw]) for n in range(3))
    s_u, s_v, c_val, c_gate = (_dot(hb, win_ref[:, n * bw:(n + 1) * bw]) for n in range(6, 10))
    gates = [_sigmoid(_dot(hb, wgate_ref[n])) for n in range(N_BRANCH)]

    ca = a_c * a_x
    ka = conva_ref.shape[0]
    ya = []
    for sl in range(nslab):
        ls = slice(sl * LANES, (sl + 1) * LANES)
        exta[sl, HALO_A:HALO_A + ts, :] = ca[:, ls]
        acc = None
        for t in range(ka):
            off = HALO_A - (ka - 1) + t
            term = conva_ref[t:t + 1, ls] * exta[sl, off:off + ts, :]
            acc = term if acc is None else acc + term
        ya.append(a_b[:, ls] * acc)
        exta[sl, 0:HALO_A, :] = exta[sl, ts:ts + HALO_A, :]
    y_a = jnp.concatenate(ya, axis=1)

    vb = _ln(s_v, sgug_ref[...], sgub_ref[...]).astype(BF16)
    ck = SGU_CHUNK
    wr = lax.broadcasted_iota(jnp.int32, (SGU_GROUPS * ck, ck), 0) & (ck - 1)
    wc = lax.broadcasted_iota(jnp.int32, (SGU_GROUPS * ck, ck), 1)
    wst = jnp.where(wc <= wr, sguw_ref[...], 0.0).astype(BF16)
    group_of_lane = lax.broadcasted_iota(jnp.int32, (ck, bw), 1) // (bw // SGU_GROUPS)
    yc = []
    for c in range(ts // ck):
        mm = _dot(wst, vb[c * ck:(c + 1) * ck, :])
        mixed = mm[0:ck]
        for g in range(1, SGU_GROUPS):
            mixed = jnp.where(group_of_lane == g, mm[g * ck:(g + 1) * ck], mixed)
        yc.append(s_u[c * ck:(c + 1) * ck, :] * (mixed + sgubias_ref[...]))
    y_c = jnp.concatenate(yc, axis=0)

    glu = c_val * _sigmoid(c_gate)
    for sl in range(nslab):
        extd[sl, HALO_D:HALO_D + ts, :] = glu[:, sl * LANES:(sl + 1) * LANES]
    cd = []
    n_pieces = nslab * (ts // CONV_ROWS)
    for sl in range(nslab):
        ls = slice(sl * LANES, (sl + 1) * LANES)
        pieces = []
        for r0 in range(0, ts, CONV_ROWS):
            acc = None
            for t in range(kd):
                off = r0 + HALO_D - (kd - 1) + t
                term = confw_ref[t:t + 1, ls] * extd[sl, off:off + CONV_ROWS, :]
                acc = term if acc is None else acc + term
            pi = len(pieces) + sl * (ts // CONV_ROWS)
            if pi % (n_pieces // N_BRANCH) == 0 and pi > 0:
                acc = _after(acc, gates[pi // (n_pieces // N_BRANCH) - 1])
            pieces.append(acc)
        cd.append(jnp.concatenate(pieces, axis=0))
        extd[sl, 0:HALO_D, :] = extd[sl, ts:ts + HALO_D, :]
    cd = _ln(jnp.concatenate(cd, axis=1), confg_ref[...], confb_ref[...])
    y_d = cd * _sigmoid(cd)

    merged = None
    for n, y in enumerate((y_a, yb_ref[0], y_c, y_d)):
        term = gates[n] * _dot(y.astype(BF16), wbr_ref[n])
        merged = term if merged is None else merged + term
    xo_ref[0] = x + _dot(merged.astype(BF16), wout_ref[...])


def _mix_call(x, y_b, layer, gmix, win, conva, sgug, sgub, sguw, sgubias,
              confw, confg, confb, wgate, wbr, wout):
    b, s, d = x.shape
    bw = conva.shape[-1]
    kd = confw.shape[1]
    ts = min(TS_MIX, s)
    nslab = bw // LANES
    tile = lambda w: pl.BlockSpec((1, ts, w), lambda i, j: (i, j, 0))
    consts = (gmix, win, conva, sgug, sgub, sguw, sgubias, confw, confg, confb,
              wgate, wbr, wout)
    own = (win, wgate, wbr, wout)
    return pl.pallas_call(
        functools.partial(_mix_body, ts=ts, bw=bw, kd=kd),
        out_shape=jax.ShapeDtypeStruct((b, s, d), F32),
        grid=(b, s // ts),
        in_specs=[tile(d), tile(bw)] + [_layer_spec(c, 0 if any(c is o for o in own) else layer) for c in consts],
        out_specs=tile(d),
        scratch_shapes=[pltpu.VMEM((nslab, HALO_A + ts, LANES), F32),
                        pltpu.VMEM((nslab, HALO_D + ts, LANES), F32)],
        compiler_params=_params(2),
        name="mix_merge",
    )(x, y_b, *consts)


def _ffn_body(x_ref, p_ref, gffn_ref, w1_ref, w2_ref, gple_ref, wpg_ref, wpp_ref,
              gnext_ref, *rest, ts, fh, bw, dils, final):
    if final:
        xo_ref, hid_ref = rest
    else:
        n_out = 3 * len(dils)
        win_ref, xo_ref = rest[:2]
        qkv_refs, stages, hid_ref = rest[2:2 + n_out], rest[2 + n_out:-1], rest[-1]
    rs = ts // FFN_SUB
    subs = [slice(h * rs, (h + 1) * rs) for h in range(FFN_SUB)]
    xs = [x_ref[0, r, :] for r in subs]
    hbs = [_rms(x, gffn_ref[...]).astype(BF16) for x in xs]
    for c in range(fh // FFN_CHUNK):
        c0 = c * FFN_CHUNK
        for r, hb in zip(subs, hbs):
            gate = _dot(hb, w1_ref[:, c0:c0 + FFN_CHUNK])
            up = _dot(hb, w1_ref[:, fh + c0:fh + c0 + FFN_CHUNK])
            hid_ref[r, c0:c0 + FFN_CHUNK] = (gate * _sigmoid(gate) * up).astype(BF16)
    xs = [x + _dot(hid_ref[r, :], w2_ref[...]) for r, x in zip(subs, xs)]
    h3s = [_rms(x, gple_ref[...]).astype(BF16) for x in xs]
    xs = [x + _sigmoid(_dot(h3, wpg_ref[...])) * _dot(p_ref[0, r, :].astype(BF16), wpp_ref[...])
          for r, x, h3 in zip(subs, xs, h3s)]
    for r, x in zip(subs, xs):
        if final:
            xo_ref[0, r, :] = _rms(x, gnext_ref[...])
        else:
            xo_ref[0, r, :] = x
            _emit_qkv(x, gnext_ref, win_ref, qkv_refs, stages, row0=r.start, bw=bw, dils=dils)


def _ffn_call(x, p, layer, gffn, w1, w2, gple, wpg, wpp, gnext, win, dils, bw):
    b, s, d = x.shape
    fh = w2.shape[1]
    final = win is None
    ts = min(TS_FFN, s)
    tile = lambda w: pl.BlockSpec((1, ts, w), lambda i, j: (i, j, 0))
    in_specs = [tile(d),
                pl.BlockSpec((None, 1, ts, p.shape[-1]), lambda i, j: (layer, i, j, 0))]
    in_specs += [_layer_spec(c, lyr) for c, lyr in ((gffn, layer), (w1, 0), (w2, 0),
                                                     (gple, layer), (wpg, 0), (wpp, 0))]
    out_shape = [jax.ShapeDtypeStruct((b, s, d), F32)]
    out_specs = [tile(d)]
    scratch = [pltpu.VMEM((ts, fh), BF16)]
    if final:
        consts = (gnext,)
        in_specs.append(_layer_spec(gnext, 0))
    else:
        consts = (gnext, win)
        in_specs += [_layer_spec(gnext, layer + 1), _layer_spec(win, 0)]
        qs, qspecs, qscratch = _qkv_outs(b, s, ts, bw, dils)
        out_shape += qs
        out_specs += qspecs
        scratch = qscratch + scratch
    outs = pl.pallas_call(
        functools.partial(_ffn_body, ts=ts, fh=fh, bw=bw, dils=dils, final=final),
        out_shape=tuple(out_shape),
        grid=(b, s // ts),
        in_specs=in_specs,
        out_specs=tuple(out_specs),
        scratch_shapes=scratch,
        compiler_params=_params(2),
        name="ffn_ple",
    )(x, p, gffn, w1, w2, gple, wpg, wpp, *consts)
    return outs[0], outs[1:]


def kernel(x, p, g_mix, w_in, conv_a, sgu_ln_g, sgu_ln_b, sgu_w, sgu_b, conf_dw,
           conf_ln_g, conf_ln_b, w_branch, w_merge_gate, w_out, g_ffn, w_ffn_in,
           w_ffn_out, g_ple, w_ple_gate, w_ple_proj, g_final):
    depth = w_in.shape[0]
    d = x.shape[-1]
    bw = d // N_BRANCH
    assert all(w // dil == ATT_BLK for w, dil in DSW_GROUPS)
    dils = tuple(dil for _, dil in DSW_GROUPS)
    blk = ATT_BLK
    head_of_col = jnp.arange(bw, dtype=jnp.int32) // (bw // ATT_HEADS)
    head_of_row = jnp.arange(ATT_HEADS * blk, dtype=jnp.int32) // blk
    hm = (head_of_row[:, None] == head_of_col[None, :]).astype(BF16)
    qi = (jnp.arange(ATT_HEADS * blk, dtype=jnp.int32) % blk)[:, None]
    ki = jnp.arange(2 * blk, dtype=jnp.int32)[None, :]
    bias = jnp.where((ki >= qi) & (ki <= qi + blk), 0.0, NEG).astype(F32)

    rows = lambda a: a.reshape(a.shape[0], 1, a.shape[-1])
    g_mix, sgu_ln_g, sgu_ln_b, conf_ln_g, conf_ln_b, g_ffn, g_ple = map(
        rows, (g_mix, sgu_ln_g, sgu_ln_b, conf_ln_g, conf_ln_b, g_ffn, g_ple))
    g_final = g_final.reshape(1, 1, d)
    sgu_w = sgu_w.reshape(depth, SGU_GROUPS * SGU_CHUNK, SGU_CHUNK)
    sgubias = jnp.repeat(jnp.swapaxes(sgu_b, 1, 2), bw // SGU_GROUPS, axis=2)
    w_f32 = (w_in, w_merge_gate, w_branch, w_out, w_ffn_in, w_ffn_out, w_ple_gate, w_ple_proj)
    shapes = [(1,) + w.shape[1:] for w in w_f32]
    w_f32 = [w.reshape(depth, -1, w.shape[-1]) for w in w_f32]
    win = w_f32[0][0:1].astype(BF16)

    qkv = _qkv_call(x, g_mix, win, 0, dils, bw)
    for i in range(depth):
        final = i == depth - 1
        casts = [(w, i) for w in w_f32[1:]] + ([] if final else [(w_f32[0], i + 1)])
        y_b, cast = _attn_call(qkv, hm, bias, dils, bw, casts)
        wgate, wbr, wout, w1, w2, wpg, wpp = (
            w.reshape(sh) for w, sh in zip(cast, shapes[1:]))
        x = _mix_call(x, y_b, i, g_mix, win, conv_a, sgu_ln_g, sgu_ln_b, sgu_w, sgubias,
                      conf_dw, conf_ln_g, conf_ln_b, wgate, wbr, wout)
        win = None if final else cast[-1].reshape(shapes[0])
        x, qkv = _ffn_call(x, p, i, g_ffn, w1, w2, g_ple, wpg, wpp,
                           g_final if final else g_mix, win, dils, bw)
    return x
```

```python
import functools

import jax
import jax.numpy as jnp
from jax import lax
from jax.experimental import pallas as pl
from jax.experimental.pallas import tpu as pltpu

F32 = jnp.float32
BF16 = jnp.bfloat16

EPS = 1e-6
N_BRANCH = 4
ATT_HEADS = 4
DSW_GROUPS = ((128, 1), (512, 4), (2048, 16))
ATT_BLK = 128
ATT_SLOTS = 8
SGU_CHUNK = 128
SGU_GROUPS = 4
LANES = 128
SUBLANES = 8
BF16_ROWS = 16
HALO_A = 8
HALO_D = 32
CONV_ROWS = 64
MERGE_ROWS = 256
FFN_CHUNK = 256
FFN_SUB = 2
QKV_SUB = 2
NEG = -float("inf")
LOG2E = 1.4426950408889634
VMEM_LIMIT = 56 * 1024 * 1024

TS_QKV = 1024
TS_MIX = 512
TS_FFN = 512


def _rms(x, g):
    ms = jnp.mean(x * x, axis=-1, keepdims=True)
    return x * lax.rsqrt(ms + EPS) * g


def _ln(x, g, b):
    mu = jnp.mean(x, axis=-1, keepdims=True)
    xc = x - mu
    var = jnp.mean(xc * xc, axis=-1, keepdims=True)
    return xc * lax.rsqrt(var + EPS) * g + b


def _sigmoid(x):
    return 1.0 / (1.0 + jnp.exp2(x * (-LOG2E)))


def _dot(a, b):
    return jnp.dot(a, b, preferred_element_type=F32)


def _after(x, anchor):
    z = pltpu.bitcast(anchor[-SUBLANES:, -LANES:], jnp.uint32)
    z = lax.shift_right_logical(lax.shift_right_logical(z, jnp.uint32(16)), jnp.uint32(16))
    never = jnp.tile(z, (x.shape[0] // SUBLANES, x.shape[1] // LANES)) != 0
    return jnp.where(never, jnp.zeros_like(x), x)


def _const_spec(shape):
    nd = len(shape)
    return pl.BlockSpec(shape, lambda *_: (0,) * nd, pipeline_mode=pl.Buffered(1))


def _layer_spec(arr, layer):
    nd = arr.ndim - 1
    return pl.BlockSpec((None,) + arr.shape[1:], lambda *_: (layer,) + (0,) * nd,
                        pipeline_mode=pl.Buffered(1))


def _cast_rows(r, steps):
    br = BF16_ROWS * pl.cdiv(pl.cdiv(r, steps), BF16_ROWS)
    while r % br:
        br += BF16_ROWS
    return br


def _params(n_grid):
    return pltpu.CompilerParams(
        dimension_semantics=("arbitrary",) * n_grid,
        vmem_limit_bytes=VMEM_LIMIT)


def _emit_qkv(x, g_ref, win_ref, outs, stages, *, row0, bw, dils):
    rs = x.shape[0]
    h = _rms(x, g_ref[...]).astype(BF16)
    qkv = _dot(h, win_ref[:, 3 * bw:6 * bw])
    nsl = bw // LANES
    scale = float((bw // ATT_HEADS) ** -0.5 * LOG2E)
    for n, dil in enumerate(dils):
        last = n == len(dils) - 1
        dst = slice(row0 // dil, (row0 + rs) // dil)
        for c in range(3 * nsl):
            t, sl = divmod(c, nsl)
            for r in range(dil):
                if n == 0:
                    rows = qkv[:, c * LANES:(c + 1) * LANES]
                    rows = rows * scale if t == 0 else rows
                else:
                    prev = dils[n - 1]
                    rows = stages[n - 1][c, r % prev,
                                         pl.ds(row0 // prev + r // prev, rs // dil,
                                               stride=dil // prev), :]
                if not last:
                    stages[n][c, r, dst, :] = rows
                lo = r * bw + sl * LANES
                outs[3 * n + t][0, dst, lo:lo + LANES] = rows.astype(BF16)


def _qkv_outs(b, s, ts, bw, dils):
    assert all(b_ % a_ == 0 for a_, b_ in zip(dils, dils[1:])) and dils[0] == 1
    out_shape, out_specs = [], []
    for dil in dils:
        for _ in range(3):
            out_shape.append(jax.ShapeDtypeStruct((b, s // dil, dil * bw), BF16))
            out_specs.append(pl.BlockSpec((1, ts // dil, dil * bw), lambda i, j: (i, j, 0)))
    scratch = [pltpu.VMEM((3 * bw // LANES, dil, ts // dil, LANES), F32) for dil in dils[:-1]]
    return out_shape, out_specs, scratch


def _qkv_body(x_ref, g_ref, win_ref, *rest, ts, bw, dils):
    n_out = 3 * len(dils)
    rs = ts // QKV_SUB
    for h in range(QKV_SUB):
        _emit_qkv(x_ref[0, h * rs:(h + 1) * rs, :], g_ref, win_ref, rest[:n_out], rest[n_out:],
                  row0=h * rs, bw=bw, dils=dils)


def _qkv_call(x, g, win, layer, dils, bw):
    b, s, d = x.shape
    ts = min(TS_QKV, s)
    out_shape, out_specs, scratch = _qkv_outs(b, s, ts, bw, dils)
    return pl.pallas_call(
        functools.partial(_qkv_body, ts=ts, bw=bw, dils=dils),
        out_shape=tuple(out_shape),
        grid=(b, s // ts),
        in_specs=[pl.BlockSpec((1, ts, d), lambda i, j: (i, j, 0)),
                  _layer_spec(g, layer), _layer_spec(win, 0)],
        out_specs=tuple(out_specs),
        scratch_shapes=scratch,
        compiler_params=_params(2),
        name="qkv_proj",
    )(x, g, win)


def _attn_block(q, kk, vv, hm, bias, first, bw):
    blk = ATT_BLK
    nh = ATT_HEADS
    dh = bw // nh
    hps = LANES // dh
    qst = jnp.concatenate([q] * nh, axis=0) * hm
    s = lax.dot_general(qst, kk, (((1,), (1,)), ((), ())), preferred_element_type=F32)
    s = s + bias
    if first is not False:
        s = jnp.concatenate([jnp.where(first, NEG, s[:, :blk]), s[:, blk:]], axis=1)
    m = jnp.max(s, axis=1, keepdims=True)
    p = jnp.exp2(s - m).astype(BF16)
    ones = jnp.ones((2 * blk, LANES), BF16)
    head_in_slab = lax.broadcasted_iota(jnp.int32, (blk, LANES), 1) // dh
    out = []
    for sl in range(bw // LANES):
        r0 = sl * hps * blk
        rhs = jnp.concatenate([vv[:, sl * LANES:(sl + 1) * LANES], ones], axis=1)
        ol = _dot(p[r0:r0 + hps * blk], rhs)
        o_s, l_s = ol[0:blk, :LANES], ol[0:blk, LANES:]
        m_s = jnp.broadcast_to(m[r0:r0 + blk], (blk, LANES))
        for hh in range(1, hps):
            sel = head_in_slab == hh
            rows = slice(hh * blk, (hh + 1) * blk)
            o_s = jnp.where(sel, ol[rows, :LANES], o_s)
            l_s = jnp.where(sel, ol[rows, LANES:], l_s)
            m_s = jnp.where(sel, m[r0 + hh * blk:r0 + (hh + 1) * blk], m_s)
        out.append((o_s, m_s, l_s))
    return out


def _attn_tiling(dil, span):
    rows = max(ATT_SLOTS // dil, 1)
    cols = min(dil, ATT_SLOTS)
    row_steps = span // (ATT_BLK * dil) // rows
    col_steps = dil // cols
    return rows, cols, row_steps, col_steps


def _attn_body(*refs, bw, dils, span, n_cast):
    ng = len(dils)
    hm_ref, bias_ref = refs[5 * ng:5 * ng + 2]
    cast_in = refs[5 * ng + 2:5 * ng + 2 + n_cast]
    yb_ref = refs[5 * ng + 2 + n_cast]
    cast_out = refs[5 * ng + 3 + n_cast:5 * ng + 3 + 2 * n_cast]
    onat, mnat, lnat = refs[5 * ng + 3 + 2 * n_cast:]
    for src, dst in zip(cast_in, cast_out):
        dst[...] = src[...].astype(BF16)
    blk = ATT_BLK
    nsl = bw // LANES
    j = pl.program_id(1)
    it = pl.program_id(2)
    hm = hm_ref[...]
    bias = bias_ref[...]
    for g, dil in enumerate(dils):
        q_ref, kc_ref, kp_ref, vc_ref, vp_ref = refs[5 * g:5 * g + 5]
        rows, cols, _, col_steps = _attn_tiling(dil, span)
        row_step = it // col_steps if col_steps > 1 else it
        col_step = it % col_steps if col_steps > 1 else 0
        for a in range(rows):
            for c in range(cols):
                cs = slice(c * bw, (c + 1) * bw)
                q = q_ref[0, a * blk:(a + 1) * blk, cs]
                if a == 0:
                    kk = jnp.concatenate([kp_ref[0, :, cs], kc_ref[0, 0:blk, cs]], axis=0)
                    vv = jnp.concatenate([vp_ref[0, :, cs], vc_ref[0, 0:blk, cs]], axis=0)
                    first = (j == 0) & (row_step == 0)
                else:
                    kk = kc_ref[0, (a - 1) * blk:(a + 1) * blk, cs]
                    vv = vc_ref[0, (a - 1) * blk:(a + 1) * blk, cs]
                    first = False
                stats = _attn_block(q, kk, vv, hm, bias, first, bw)
                start = (row_step * rows + a) * (blk * dil) + col_step * cols + c
                if dil == 1:
                    idx = pl.ds(pl.multiple_of(start, blk), blk)
                else:
                    idx = pl.ds(start, blk, stride=dil)
                for sl, (o_s, m_s, l_s) in enumerate(stats):
                    onat[g, sl, idx, :] = o_s
                    mnat[g, sl, idx, :] = m_s
                    lnat[g, sl, idx, :] = l_s

    @pl.when(it == pl.num_programs(2) - 1)
    def _():
        def merge(c, carry):
            rows = pl.ds(pl.multiple_of(c * MERGE_ROWS, MERGE_ROWS), MERGE_ROWS)
            for sl in range(nsl):
                ms = [mnat[g, sl, rows, :] for g in range(ng)]
                mm = functools.reduce(jnp.maximum, ms)
                ws = [jnp.exp2(m - mm) for m in ms]
                num = sum(w * onat[g, sl, rows, :] for g, w in enumerate(ws))
                den = sum(w * lnat[g, sl, rows, :] for g, w in enumerate(ws))
                yb_ref[0, rows, sl * LANES:(sl + 1) * LANES] = (num / den).astype(BF16)
            return carry
        lax.fori_loop(0, span // MERGE_ROWS, merge, 0)


def _attn_call(qkv, hm, bias, dils, bw, casts):
    b, s, _ = qkv[0].shape
    blk = ATT_BLK
    span = blk * max(dils)
    nsteps = span // blk // ATT_SLOTS
    in_specs, args = [], []
    for g, dil in enumerate(dils):
        rows, cols, row_steps, col_steps = _attn_tiling(dil, span)
        assert row_steps * col_steps == nsteps

        def cur(i, j, it, rs=row_steps, cst=col_steps):
            return (i, j * rs + it // cst, it % cst)

        def prev(i, j, it, rs=row_steps, cst=col_steps, r=rows):
            return (i, jnp.maximum((j * rs + it // cst) * r - 1, 0), it % cst)

        q, k, v = qkv[3 * g:3 * g + 3]
        cur_spec = pl.BlockSpec((1, rows * blk, cols * bw), cur)
        prev_spec = pl.BlockSpec((1, blk, cols * bw), prev)
        in_specs += [cur_spec, cur_spec, prev_spec, cur_spec, prev_spec]
        args += [q, k, k, v, v]
    nsl = bw // LANES
    n_tiles = s // span
    out_shape = [jax.ShapeDtypeStruct((b, s, bw), BF16)]
    out_specs = [pl.BlockSpec((1, span, bw), lambda i, j, u: (i, j, 0))]
    cast_specs = []
    for w, layer in casts:
        _, r, c = w.shape
        br = _cast_rows(r, b * n_tiles * nsteps)
        at = lambda i, j, u, last=r // br - 1: jnp.minimum((i * n_tiles + j) * nsteps + u, last)
        cast_specs.append(pl.BlockSpec((None, br, c),
                                       lambda i, j, u, at=at, lyr=layer: (lyr, at(i, j, u), 0)))
        out_specs.append(pl.BlockSpec((br, c), lambda i, j, u, at=at: (at(i, j, u), 0)))
        out_shape.append(jax.ShapeDtypeStruct((r, c), BF16))
    outs = pl.pallas_call(
        functools.partial(_attn_body, bw=bw, dils=dils, span=span, n_cast=len(casts)),
        out_shape=tuple(out_shape),
        grid=(b, n_tiles, nsteps),
        in_specs=in_specs + [_const_spec(hm.shape), _const_spec(bias.shape)] + cast_specs,
        out_specs=tuple(out_specs),
        scratch_shapes=[pltpu.VMEM((len(dils), nsl, span, LANES), F32)] * 3,
        compiler_params=_params(3),
        name="band_attn",
    )(*args, hm, bias, *[w for w, _ in casts])
    return outs[0], outs[1:]


def _mix_body(x_ref, yb_ref,
              gmix_ref, win_ref, conva_ref, sgug_ref, sgub_ref, sguw_ref, sgubias_ref,
              confw_ref, confg_ref, confb_ref, wgate_ref, wbr_ref, wout_ref,
              xo_ref, exta, extd, *, ts, bw, kd):
    nslab = bw // LANES

    @pl.when(pl.program_id(1) == 0)
    def _():
        exta[:, 0:HALO_A, :] = jnp.zeros((nslab, HALO_A, LANES), F32)
        extd[:, 0:HALO_D, :] = jnp.zeros((nslab, HALO_D, LANES), F32)

    x = x_ref[0]
    hb = _rms(x, gmix_ref[...]).astype(BF16)
    a_b, a_c, a_x = (_dot(hb, win_ref[:, n * bw:(n + 1) * bw]) for n in range(3))
    s_u, s_v, c_val, c_gate = (_dot(hb, win_ref[:, n * bw:(n + 1) * bw]) for n in range(6, 10))
    gates = [_sigmoid(_dot(hb, wgate_ref[n])) for n in range(N_BRANCH)]

    ca = a_c * a_x
    ka = conva_ref.shape[0]
    ya = []
    for sl in range(nslab):
        ls = slice(sl * LANES, (sl + 1) * LANES)
        exta[sl, HALO_A:HALO_A + ts, :] = ca[:, ls]
        acc = None
        for t in range(ka):
            off = HALO_A - (ka - 1) + t
            term = conva_ref[t:t + 1, ls] * exta[sl, off:off + ts, :]
            acc = term if acc is None else acc + term
        ya.append(a_b[:, ls] * acc)
        exta[sl, 0:HALO_A, :] = exta[sl, ts:ts + HALO_A, :]
    y_a = jnp.concatenate(ya, axis=1)

    vb = _ln(s_v, sgug_ref[...], sgub_ref[...]).astype(BF16)
    ck = SGU_CHUNK
    wr = lax.broadcasted_iota(jnp.int32, (SGU_GROUPS * ck, ck), 0) & (ck - 1)
    wc = lax.broadcasted_iota(jnp.int32, (SGU_GROUPS * ck, ck), 1)
    wst = jnp.where(wc <= wr, sguw_ref[...], 0.0).astype(BF16)
    group_of_lane = lax.broadcasted_iota(jnp.int32, (ck, bw), 1) // (bw // SGU_GROUPS)
    yc = []
    for c in range(ts // ck):
        mm = _dot(wst, vb[c * ck:(c + 1) * ck, :])
        mixed = mm[0:ck]
        for g in range(1, SGU_GROUPS):
            mixed = jnp.where(group_of_lane == g, mm[g * ck:(g + 1) * ck], mixed)
        yc.append(s_u[c * ck:(c + 1) * ck, :] * (mixed + sgubias_ref[...]))
    y_c = jnp.concatenate(yc, axis=0)

    glu = c_val * _sigmoid(c_gate)
    for sl in range(nslab):
        extd[sl, HALO_D:HALO_D + ts, :] = glu[:, sl * LANES:(sl + 1) * LANES]
    cd = []
    n_pieces = nslab * (ts // CONV_ROWS)
    for sl in range(nslab):
        ls = slice(sl * LANES, (sl + 1) * LANES)
        pieces = []
        for r0 in range(0, ts, CONV_ROWS):
            acc = None
            for t in range(kd):
                off = r0 + HALO_D - (kd - 1) + t
                term = confw_ref[t:t + 1, ls] * extd[sl, off:off + CONV_ROWS, :]
                acc = term if acc is None else acc + term
            pi = len(pieces) + sl * (ts // CONV_ROWS)
            if pi % (n_pieces // N_BRANCH) == 0 and pi > 0:
                acc = _after(acc, gates[pi // (n_pieces // N_BRANCH) - 1])
            pieces.append(acc)
        cd.append(jnp.concatenate(pieces, axis=0))
        extd[sl, 0:HALO_D, :] = extd[sl, ts:ts + HALO_D, :]
    cd = _ln(jnp.concatenate(cd, axis=1), confg_ref[...], confb_ref[...])
    y_d = cd * _sigmoid(cd)

    merged = None
    for n, y in enumerate((y_a, yb_ref[0], y_c, y_d)):
        term = gates[n] * _dot(y.astype(BF16), wbr_ref[n])
        merged = term if merged is None else merged + term
    xo_ref[0] = x + _dot(merged.astype(BF16), wout_ref[...])


def _mix_call(x, y_b, layer, gmix, win, conva, sgug, sgub, sguw, sgubias,
              confw, confg, confb, wgate, wbr, wout):
    b, s, d = x.shape
    bw = conva.shape[-1]
    kd = confw.shape[1]
    ts = min(TS_MIX, s)
    nslab = bw // LANES
    tile = lambda w: pl.BlockSpec((1, ts, w), lambda i, j: (i, j, 0))
    consts = (gmix, win, conva, sgug, sgub, sguw, sgubias, confw, confg, confb,
              wgate, wbr, wout)
    own = (win, wgate, wbr, wout)
    return pl.pallas_call(
        functools.partial(_mix_body, ts=ts, bw=bw, kd=kd),
        out_shape=jax.ShapeDtypeStruct((b, s, d), F32),
        grid=(b, s // ts),
        in_specs=[tile(d), tile(bw)] + [_layer_spec(c, 0 if any(c is o for o in own) else layer) for c in consts],
        out_specs=tile(d),
        scratch_shapes=[pltpu.VMEM((nslab, HALO_A + ts, LANES), F32),
                        pltpu.VMEM((nslab, HALO_D + ts, LANES), F32)],
        compiler_params=_params(2),
        name="mix_merge",
    )(x, y_b, *consts)


def _ffn_body(x_ref, p_ref, gffn_ref, w1_ref, w2_ref, gple_ref, wpg_ref, wpp_ref,
              gnext_ref, *rest, ts, fh, bw, dils, final):
    if final:
        xo_ref, hid_ref = rest
    else:
        n_out = 3 * len(dils)
        win_ref, xo_ref = rest[:2]
        qkv_refs, stages, hid_ref = rest[2:2 + n_out], rest[2 + n_out:-1], rest[-1]
    rs = ts // FFN_SUB
    subs = [slice(h * rs, (h + 1) * rs) for h in range(FFN_SUB)]
    xs = [x_ref[0, r, :] for r in subs]
    hbs = [_rms(x, gffn_ref[...]).astype(BF16) for x in xs]
    for c in range(fh // FFN_CHUNK):
        c0 = c * FFN_CHUNK
        for r, hb in zip(subs, hbs):
            gate = _dot(hb, w1_ref[:, c0:c0 + FFN_CHUNK])
            up = _dot(hb, w1_ref[:, fh + c0:fh + c0 + FFN_CHUNK])
            hid_ref[r, c0:c0 + FFN_CHUNK] = (gate * _sigmoid(gate) * up).astype(BF16)
    xs = [x + _dot(hid_ref[r, :], w2_ref[...]) for r, x in zip(subs, xs)]
    h3s = [_rms(x, gple_ref[...]).astype(BF16) for x in xs]
    xs = [x + _sigmoid(_dot(h3, wpg_ref[...])) * _dot(p_ref[0, r, :].astype(BF16), wpp_ref[...])
          for r, x, h3 in zip(subs, xs, h3s)]
    for r, x in zip(subs, xs):
        if final:
            xo_ref[0, r, :] = _rms(x, gnext_ref[...])
        else:
            xo_ref[0, r, :] = x
            _emit_qkv(x, gnext_ref, win_ref, qkv_refs, stages, row0=r.start, bw=bw, dils=dils)


def _ffn_call(x, p, layer, gffn, w1, w2, gple, wpg, wpp, gnext, win, dils, bw):
    b, s, d = x.shape
    fh = w2.shape[1]
    final = win is None
    ts = min(TS_FFN, s)
    tile = lambda w: pl.BlockSpec((1, ts, w), lambda i, j: (i, j, 0))
    in_specs = [tile(d),
                pl.BlockSpec((None, 1, ts, p.shape[-1]), lambda i, j: (layer, i, j, 0))]
    in_specs += [_layer_spec(c, lyr) for c, lyr in ((gffn, layer), (w1, 0), (w2, 0),
                                                     (gple, layer), (wpg, 0), (wpp, 0))]
    out_shape = [jax.ShapeDtypeStruct((b, s, d), F32)]
    out_specs = [tile(d)]
    scratch = [pltpu.VMEM((ts, fh), BF16)]
    if final:
        consts = (gnext,)
        in_specs.append(_layer_spec(gnext, 0))
    else:
        consts = (gnext, win)
        in_specs += [_layer_spec(gnext, layer + 1), _layer_spec(win, 0)]
        qs, qspecs, qscratch = _qkv_outs(b, s, ts, bw, dils)
        out_shape += qs
        out_specs += qspecs
        scratch = qscratch + scratch
    outs = pl.pallas_call(
        functools.partial(_ffn_body, ts=ts, fh=fh, bw=bw, dils=dils, final=final),
        out_shape=tuple(out_shape),
        grid=(b, s // ts),
        in_specs=in_specs,
        out_specs=tuple(out_specs),
        scratch_shapes=scratch,
        compiler_params=_params(2),
        name="ffn_ple",
    )(x, p, gffn, w1, w2, gple, wpg, wpp, *consts)
    return outs[0], outs[1:]


def kernel(x, p, g_mix, w_in, conv_a, sgu_ln_g, sgu_ln_b, sgu_w, sgu_b, conf_dw,
           conf_ln_g, conf_ln_b, w_branch, w_merge_gate, w_out, g_ffn, w_ffn_in,
           w_ffn_out, g_ple, w_ple_gate, w_ple_proj, g_final):
    depth = w_in.shape[0]
    d = x.shape[-1]
    bw = d // N_BRANCH
    assert all(w // dil == ATT_BLK for w, dil in DSW_GROUPS)
    dils = tuple(dil for _, dil in DSW_GROUPS)
    blk = ATT_BLK
    head_of_col = jnp.arange(bw, dtype=jnp.int32) // (bw // ATT_HEADS)
    head_of_row = jnp.arange(ATT_HEADS * blk, dtype=jnp.int32) // blk
    hm = (head_of_row[:, None] == head_of_col[None, :]).astype(BF16)
    qi = (jnp.arange(ATT_HEADS * blk, dtype=jnp.int32) % blk)[:, None]
    ki = jnp.arange(2 * blk, dtype=jnp.int32)[None, :]
    bias = jnp.where((ki >= qi) & (ki <= qi + blk), 0.0, NEG).astype(F32)

    rows = lambda a: a.reshape(a.shape[0], 1, a.shape[-1])
    g_mix, sgu_ln_g, sgu_ln_b, conf_ln_g, conf_ln_b, g_ffn, g_ple = map(
        rows, (g_mix, sgu_ln_g, sgu_ln_b, conf_ln_g, conf_ln_b, g_ffn, g_ple))
    g_final = g_final.reshape(1, 1, d)
    sgu_w = sgu_w.reshape(depth, SGU_GROUPS * SGU_CHUNK, SGU_CHUNK)
    sgubias = jnp.repeat(jnp.swapaxes(sgu_b, 1, 2), bw // SGU_GROUPS, axis=2)
    w_f32 = (w_in, w_merge_gate, w_branch, w_out, w_ffn_in, w_ffn_out, w_ple_gate, w_ple_proj)
    shapes = [(1,) + w.shape[1:] for w in w_f32]
    w_f32 = [w.reshape(depth, -1, w.shape[-1]) for w in w_f32]
    win = w_f32[0][0:1].astype(BF16)

    qkv = _qkv_call(x, g_mix, win, 0, dils, bw)
    for i in range(depth):
        final = i == depth - 1
        casts = [(w, i) for w in w_f32[1:]] + ([] if final else [(w_f32[0], i + 1)])
        y_b, cast = _attn_call(qkv, hm, bias, dils, bw, casts)
        wgate, wbr, wout, w1, w2, wpg, wpp = (
            w.reshape(sh) for w, sh in zip(cast, shapes[1:]))
        x = _mix_call(x, y_b, i, g_mix, win, conv_a, sgu_ln_g, sgu_ln_b, sgu_w, sgubias,
                      conf_dw, conf_ln_g, conf_ln_b, wgate, wbr, wout)
        win = None if final else cast[-1].reshape(shapes[0])
        x, qkv = _ffn_call(x, p, i, g_ffn, w1, w2, g_ple, wpg, wpp,
                           g_final if final else g_mix, win, dils, bw)
    return x
```

```python
import functools

import jax
import jax.numpy as jnp
from jax import lax
from jax.experimental import pallas as pl
from jax.experimental.pallas import tpu as pltpu

F32 = jnp.float32
BF16 = jnp.bfloat16

EPS = 1e-6
N_BRANCH = 4
ATT_HEADS = 4
DSW_GROUPS = ((128, 1), (512, 4), (2048, 16))
ATT_BLK = 128
ATT_SLOTS = 8
SGU_CHUNK = 128
SGU_GROUPS = 4
LANES = 128
SUBLANES = 8
BF16_ROWS = 16
HALO_A = 8
HALO_D = 32
CONV_ROWS = 64
MERGE_ROWS = 256
FFN_CHUNK = 256
FFN_SUB = 2
QKV_SUB = 2
NEG = -float("inf")
LOG2E = 1.4426950408889634
VMEM_LIMIT = 56 * 1024 * 1024

TS_QKV = 1024
TS_MIX = 512
TS_FFN = 512


def _rms(x, g):
    ms = jnp.mean(x * x, axis=-1, keepdims=True)
    return x * lax.rsqrt(ms + EPS) * g


def _ln(x, g, b):
    mu = jnp.mean(x, axis=-1, keepdims=True)
    xc = x - mu
    var = jnp.mean(xc * xc, axis=-1, keepdims=True)
    return xc * lax.rsqrt(var + EPS) * g + b


def _sigmoid(x):
    return 1.0 / (1.0 + jnp.exp2(x * (-LOG2E)))


def _dot(a, b):
    return jnp.dot(a, b, preferred_element_type=F32)


def _after(x, anchor):
    z = pltpu.bitcast(anchor[-SUBLANES:, -LANES:], jnp.uint32)
    z = lax.shift_right_logical(lax.shift_right_logical(z, jnp.uint32(16)), jnp.uint32(16))
    never = jnp.tile(z, (x.shape[0] // SUBLANES, x.shape[1] // LANES)) != 0
    return jnp.where(never, jnp.zeros_like(x), x)


def _const_spec(shape):
    nd = len(shape)
    return pl.BlockSpec(shape, lambda *_: (0,) * nd, pipeline_mode=pl.Buffered(1))


def _layer_spec(arr, layer):
    nd = arr.ndim - 1
    return pl.BlockSpec((None,) + arr.shape[1:], lambda *_: (layer,) + (0,) * nd,
                        pipeline_mode=pl.Buffered(1))


def _cast_rows(r, steps):
    br = BF16_ROWS * pl.cdiv(pl.cdiv(r, steps), BF16_ROWS)
    while r % br:
        br += BF16_ROWS
    return br


def _params(n_grid):
    return pltpu.CompilerParams(
        dimension_semantics=("arbitrary",) * n_grid,
        vmem_limit_bytes=VMEM_LIMIT)


def _emit_qkv(x, g_ref, win_ref, outs, stages, *, row0, bw, dils):
    rs = x.shape[0]
    h = _rms(x, g_ref[...]).astype(BF16)
    qkv = _dot(h, win_ref[:, 3 * bw:6 * bw])
    nsl = bw // LANES
    scale = float((bw // ATT_HEADS) ** -0.5 * LOG2E)
    for n, dil in enumerate(dils):
        last = n == len(dils) - 1
        dst = slice(row0 // dil, (row0 + rs) // dil)
        for c in range(3 * nsl):
            t, sl = divmod(c, nsl)
            for r in range(dil):
                if n == 0:
                    rows = qkv[:, c * LANES:(c + 1) * LANES]
                    rows = rows * scale if t == 0 else rows
                else:
                    prev = dils[n - 1]
                    rows = stages[n - 1][c, r % prev,
                                         pl.ds(row0 // prev + r // prev, rs // dil,
                                               stride=dil // prev), :]
                if not last:
                    stages[n][c, r, dst, :] = rows
                lo = r * bw + sl * LANES
                outs[3 * n + t][0, dst, lo:lo + LANES] = rows.astype(BF16)


def _qkv_outs(b, s, ts, bw, dils):
    assert all(b_ % a_ == 0 for a_, b_ in zip(dils, dils[1:])) and dils[0] == 1
    out_shape, out_specs = [], []
    for dil in dils:
        for _ in range(3):
            out_shape.append(jax.ShapeDtypeStruct((b, s // dil, dil * bw), BF16))
            out_specs.append(pl.BlockSpec((1, ts // dil, dil * bw), lambda i, j: (i, j, 0)))
    scratch = [pltpu.VMEM((3 * bw // LANES, dil, ts // dil, LANES), F32) for dil in dils[:-1]]
    return out_shape, out_specs, scratch


def _qkv_body(x_ref, g_ref, win_ref, *rest, ts, bw, dils):
    n_out = 3 * len(dils)
    rs = ts // QKV_SUB
    for h in range(QKV_SUB):
        _emit_qkv(x_ref[0, h * rs:(h + 1) * rs, :], g_ref, win_ref, rest[:n_out], rest[n_out:],
                  row0=h * rs, bw=bw, dils=dils)


def _qkv_call(x, g, win, layer, dils, bw):
    b, s, d = x.shape
    ts = min(TS_QKV, s)
    out_shape, out_specs, scratch = _qkv_outs(b, s, ts, bw, dils)
    return pl.pallas_call(
        functools.partial(_qkv_body, ts=ts, bw=bw, dils=dils),
        out_shape=tuple(out_shape),
        grid=(b, s // ts),
        in_specs=[pl.BlockSpec((1, ts, d), lambda i, j: (i, j, 0)),
                  _layer_spec(g, layer), _layer_spec(win, 0)],
        out_specs=tuple(out_specs),
        scratch_shapes=scratch,
        compiler_params=_params(2),
        name="qkv_proj",
    )(x, g, win)


def _attn_block(q, kk, vv, hm, bias, first, bw):
    blk = ATT_BLK
    nh = ATT_HEADS
    dh = bw // nh
    hps = LANES // dh
    qst = jnp.concatenate([q] * nh, axis=0) * hm
    s = lax.dot_general(qst, kk, (((1,), (1,)), ((), ())), preferred_element_type=F32)
    s = s + bias
    if first is not False:
        s = jnp.concatenate([jnp.where(first, NEG, s[:, :blk]), s[:, blk:]], axis=1)
    m = jnp.max(s, axis=1, keepdims=True)
    p = jnp.exp2(s - m).astype(BF16)
    ones = jnp.ones((2 * blk, LANES), BF16)
    head_in_slab = lax.broadcasted_iota(jnp.int32, (blk, LANES), 1) // dh
    out = []
    for sl in range(bw // LANES):
        r0 = sl * hps * blk
        rhs = jnp.concatenate([vv[:, sl * LANES:(sl + 1) * LANES], ones], axis=1)
        ol = _dot(p[r0:r0 + hps * blk], rhs)
        o_s, l_s = ol[0:blk, :LANES], ol[0:blk, LANES:]
        m_s = jnp.broadcast_to(m[r0:r0 + blk], (blk, LANES))
        for hh in range(1, hps):
            sel = head_in_slab == hh
            rows = slice(hh * blk, (hh + 1) * blk)
            o_s = jnp.where(sel, ol[rows, :LANES], o_s)
            l_s = jnp.where(sel, ol[rows, LANES:], l_s)
            m_s = jnp.where(sel, m[r0 + hh * blk:r0 + (hh + 1) * blk], m_s)
        out.append((o_s, m_s, l_s))
    return out


def _attn_tiling(dil, span):
    rows = max(ATT_SLOTS // dil, 1)
    cols = min(dil, ATT_SLOTS)
    row_steps = span // (ATT_BLK * dil) // rows
    col_steps = dil // cols
    return rows, cols, row_steps, col_steps


def _attn_body(*refs, bw, dils, span, n_cast):
    ng = len(dils)
    hm_ref, bias_ref = refs[5 * ng:5 * ng + 2]
    cast_in = refs[5 * ng + 2:5 * ng + 2 + n_cast]
    yb_ref = refs[5 * ng + 2 + n_cast]
    cast_out = refs[5 * ng + 3 + n_cast:5 * ng + 3 + 2 * n_cast]
    onat, mnat, lnat = refs[5 * ng + 3 + 2 * n_cast:]
    for src, dst in zip(cast_in, cast_out):
        dst[...] = src[...].astype(BF16)
    blk = ATT_BLK
    nsl = bw // LANES
    j = pl.program_id(1)
    it = pl.program_id(2)
    hm = hm_ref[...]
    bias = bias_ref[...]
    for g, dil in enumerate(dils):
        q_ref, kc_ref, kp_ref, vc_ref, vp_ref = refs[5 * g:5 * g + 5]
        rows, cols, _, col_steps = _attn_tiling(dil, span)
        row_step = it // col_steps if col_steps > 1 else it
        col_step = it % col_steps if col_steps > 1 else 0
        for a in range(rows):
            for c in range(cols):
                cs = slice(c * bw, (c + 1) * bw)
                q = q_ref[0, a * blk:(a + 1) * blk, cs]
                if a == 0:
                    kk = jnp.concatenate([kp_ref[0, :, cs], kc_ref[0, 0:blk, cs]], axis=0)
                    vv = jnp.concatenate([vp_ref[0, :, cs], vc_ref[0, 0:blk, cs]], axis=0)
                    first = (j == 0) & (row_step == 0)
                else:
                    kk = kc_ref[0, (a - 1) * blk:(a + 1) * blk, cs]
                    vv = vc_ref[0, (a - 1) * blk:(a + 1) * blk, cs]
                    first = False
                stats = _attn_block(q, kk, vv, hm, bias, first, bw)
                start = (row_step * rows + a) * (blk * dil) + col_step * cols + c
                if dil == 1:
                    idx = pl.ds(pl.multiple_of(start, blk), blk)
                else:
                    idx = pl.ds(start, blk, stride=dil)
                for sl, (o_s, m_s, l_s) in enumerate(stats):
                    onat[g, sl, idx, :] = o_s
                    mnat[g, sl, idx, :] = m_s
                    lnat[g, sl, idx, :] = l_s

    @pl.when(it == pl.num_programs(2) - 1)
    def _():
        def merge(c, carry):
            rows = pl.ds(pl.multiple_of(c * MERGE_ROWS, MERGE_ROWS), MERGE_ROWS)
            for sl in range(nsl):
                ms = [mnat[g, sl, rows, :] for g in range(ng)]
                mm = functools.reduce(jnp.maximum, ms)
                ws = [jnp.exp2(m - mm) for m in ms]
                num = sum(w * onat[g, sl, rows, :] for g, w in enumerate(ws))
                den = sum(w * lnat[g, sl, rows, :] for g, w in enumerate(ws))
                yb_ref[0, rows, sl * LANES:(sl + 1) * LANES] = (num / den).astype(BF16)
            return carry
        lax.fori_loop(0, span // MERGE_ROWS, merge, 0)


def _attn_call(qkv, hm, bias, dils, bw, casts):
    b, s, _ = qkv[0].shape
    blk = ATT_BLK
    span = blk * max(dils)
    nsteps = span // blk // ATT_SLOTS
    in_specs, args = [], []
    for g, dil in enumerate(dils):
        rows, cols, row_steps, col_steps = _attn_tiling(dil, span)
        assert row_steps * col_steps == nsteps

        def cur(i, j, it, rs=row_steps, cst=col_steps):
            return (i, j * rs + it // cst, it % cst)

        def prev(i, j, it, rs=row_steps, cst=col_steps, r=rows):
            return (i, jnp.maximum((j * rs + it // cst) * r - 1, 0), it % cst)

        q, k, v = qkv[3 * g:3 * g + 3]
        cur_spec = pl.BlockSpec((1, rows * blk, cols * bw), cur)
        prev_spec = pl.BlockSpec((1, blk, cols * bw), prev)
        in_specs += [cur_spec, cur_spec, prev_spec, cur_spec, prev_spec]
        args += [q, k, k, v, v]
    nsl = bw // LANES
    n_tiles = s // span
    out_shape = [jax.ShapeDtypeStruct((b, s, bw), BF16)]
    out_specs = [pl.BlockSpec((1, span, bw), lambda i, j, u: (i, j, 0))]
    cast_specs = []
    for w, layer in casts:
        _, r, c = w.shape
        br = _cast_rows(r, b * n_tiles * nsteps)
        at = lambda i, j, u, last=r // br - 1: jnp.minimum((i * n_tiles + j) * nsteps + u, last)
        cast_specs.append(pl.BlockSpec((None, br, c),
                                       lambda i, j, u, at=at, lyr=layer: (lyr, at(i, j, u), 0)))
        out_specs.append(pl.BlockSpec((br, c), lambda i, j, u, at=at: (at(i, j, u), 0)))
        out_shape.append(jax.ShapeDtypeStruct((r, c), BF16))
    outs = pl.pallas_call(
        functools.partial(_attn_body, bw=bw, dils=dils, span=span, n_cast=len(casts)),
        out_shape=tuple(out_shape),
        grid=(b, n_tiles, nsteps),
        in_specs=in_specs + [_const_spec(hm.shape), _const_spec(bias.shape)] + cast_specs,
        out_specs=tuple(out_specs),
        scratch_shapes=[pltpu.VMEM((len(dils), nsl, span, LANES), F32)] * 3,
        compiler_params=_params(3),
        name="band_attn",
    )(*args, hm, bias, *[w for w, _ in casts])
    return outs[0], outs[1:]


def _mix_body(x_ref, yb_ref,
              gmix_ref, win_ref, conva_ref, sgug_ref, sgub_ref, sguw_ref, sgubias_ref,
              confw_ref, confg_ref, confb_ref, wgate_ref, wbr_ref, wout_ref,
              xo_ref, exta, extd, *, ts, bw, kd):
    nslab = bw // LANES

    @pl.when(pl.program_id(1) == 0)
    def _():
        exta[:, 0:HALO_A, :] = jnp.zeros((nslab, HALO_A, LANES), F32)
        extd[:, 0:HALO_D, :] = jnp.zeros((nslab, HALO_D, LANES), F32)

    x = x_ref[0]
    hb = _rms(x, gmix_ref[...]).astype(BF16)
    a_b, a_c, a_x = (_dot(hb, win_ref[:, n * bw:(n + 1) * bw]) for n in range(3))
    s_u, s_v, c_val, c_gate = (_dot(hb, win_ref[:, n * bw:(n + 1) * bw]) for n in range(6, 10))
    gates = [_sigmoid(_dot(hb, wgate_ref[n])) for n in range(N_BRANCH)]

    ca = a_c * a_x
    ka = conva_ref.shape[0]
    ya = []
    for sl in range(nslab):
        ls = slice(sl * LANES, (sl + 1) * LANES)
        exta[sl, HALO_A:HALO_A + ts, :] = ca[:, ls]
        acc = None
        for t in range(ka):
            off = HALO_A - (ka - 1) + t
            term = conva_ref[t:t + 1, ls] * exta[sl, off:off + ts, :]
            acc = term if acc is None else acc + term
        ya.append(a_b[:, ls] * acc)
        exta[sl, 0:HALO_A, :] = exta[sl, ts:ts + HALO_A, :]
    y_a = jnp.concatenate(ya, axis=1)

    vb = _ln(s_v, sgug_ref[...], sgub_ref[...]).astype(BF16)
    ck = SGU_CHUNK
    wr = lax.broadcasted_iota(jnp.int32, (SGU_GROUPS * ck, ck), 0) & (ck - 1)
    wc = lax.broadcasted_iota(jnp.int32, (SGU_GROUPS * ck, ck), 1)
    wst = jnp.where(wc <= wr, sguw_ref[...], 0.0).astype(BF16)
    group_of_lane = lax.broadcasted_iota(jnp.int32, (ck, bw), 1) // (bw // SGU_GROUPS)
    yc = []
    for c in range(ts // ck):
        mm = _dot(wst, vb[c * ck:(c + 1) * ck, :])
        mixed = mm[0:ck]
        for g in range(1, SGU_GROUPS):
            mixed = jnp.where(group_of_lane == g, mm[g * ck:(g + 1) * ck], mixed)
        yc.append(s_u[c * ck:(c + 1) * ck, :] * (mixed + sgubias_ref[...]))
    y_c = jnp.concatenate(yc, axis=0)

    glu = c_val * _sigmoid(c_gate)
    for sl in range(nslab):
        extd[sl, HALO_D:HALO_D + ts, :] = glu[:, sl * LANES:(sl + 1) * LANES]
    cd = []
    n_pieces = nslab * (ts // CONV_ROWS)
    for sl in range(nslab):
        ls = slice(sl * LANES, (sl + 1) * LANES)
        pieces = []
        for r0 in range(0, ts, CONV_ROWS):
            acc = None
            for t in range(kd):
                off = r0 + HALO_D - (kd - 1) + t
                term = confw_ref[t:t + 1, ls] * extd[sl, off:off + CONV_ROWS, :]
                acc = term if acc is None else acc + term
            pi = len(pieces) + sl * (ts // CONV_ROWS)
            if pi % (n_pieces // N_BRANCH) == 0 and pi > 0:
                acc = _after(acc, gates[pi // (n_pieces // N_BRANCH) - 1])
            pieces.append(acc)
        cd.append(jnp.concatenate(pieces, axis=0))
        extd[sl, 0:HALO_D, :] = extd[sl, ts:ts + HALO_D, :]
    cd = _ln(jnp.concatenate(cd, axis=1), confg_ref[...], confb_ref[...])
    y_d = cd * _sigmoid(cd)

    ys = [y.astype(BF16) for y in (y_a, yb_ref[0], y_c, y_d)]
    slabs = []
    for c0 in range(0, x.shape[1], 256):
        merged = None
        for n, y in enumerate(ys):
            term = gates[n][:, c0:c0 + 256] * _dot(y, wbr_ref[n, :, c0:c0 + 256])
            merged = term if merged is None else merged + term
        slabs.append(merged.astype(BF16))
    xo_ref[0] = x + _dot(jnp.concatenate(slabs, axis=1), wout_ref[...])


def _mix_call(x, y_b, layer, gmix, win, conva, sgug, sgub, sguw, sgubias,
              confw, confg, confb, wgate, wbr, wout):
    b, s, d = x.shape
    bw = conva.shape[-1]
    kd = confw.shape[1]
    ts = min(TS_MIX, s)
    nslab = bw // LANES
    tile = lambda w: pl.BlockSpec((1, ts, w), lambda i, j: (i, j, 0))
    consts = (gmix, win, conva, sgug, sgub, sguw, sgubias, confw, confg, confb,
              wgate, wbr, wout)
    own = (win, wgate, wbr, wout)
    return pl.pallas_call(
        functools.partial(_mix_body, ts=ts, bw=bw, kd=kd),
        out_shape=jax.ShapeDtypeStruct((b, s, d), F32),
        grid=(b, s // ts),
        in_specs=[tile(d), tile(bw)] + [_layer_spec(c, 0 if any(c is o for o in own) else layer) for c in consts],
        out_specs=tile(d),
        scratch_shapes=[pltpu.VMEM((nslab, HALO_A + ts, LANES), F32),
                        pltpu.VMEM((nslab, HALO_D + ts, LANES), F32)],
        compiler_params=_params(2),
        name="mix_merge",
    )(x, y_b, *consts)


def _ffn_body(x_ref, p_ref, gffn_ref, w1_ref, w2_ref, gple_ref, wpg_ref, wpp_ref,
              gnext_ref, *rest, ts, fh, bw, dils, final):
    if final:
        xo_ref, hid_ref = rest
    else:
        n_out = 3 * len(dils)
        win_ref, xo_ref = rest[:2]
        qkv_refs, stages, hid_ref = rest[2:2 + n_out], rest[2 + n_out:-1], rest[-1]
    rs = ts // FFN_SUB
    subs = [slice(h * rs, (h + 1) * rs) for h in range(FFN_SUB)]
    xs = [x_ref[0, r, :] for r in subs]
    hbs = [_rms(x, gffn_ref[...]).astype(BF16) for x in xs]
    for c in range(fh // FFN_CHUNK):
        c0 = c * FFN_CHUNK
        for r, hb in zip(subs, hbs):
            gate = _dot(hb, w1_ref[:, c0:c0 + FFN_CHUNK])
            up = _dot(hb, w1_ref[:, fh + c0:fh + c0 + FFN_CHUNK])
            hid_ref[r, c0:c0 + FFN_CHUNK] = (gate * _sigmoid(gate) * up).astype(BF16)
    xs = [x + _dot(hid_ref[r, :], w2_ref[...]) for r, x in zip(subs, xs)]
    h3s = [_rms(x, gple_ref[...]).astype(BF16) for x in xs]
    xs = [x + _sigmoid(_dot(h3, wpg_ref[...])) * _dot(p_ref[0, r, :].astype(BF16), wpp_ref[...])
          for r, x, h3 in zip(subs, xs, h3s)]
    for r, x in zip(subs, xs):
        if final:
            xo_ref[0, r, :] = _rms(x, gnext_ref[...])
        else:
            xo_ref[0, r, :] = x
            _emit_qkv(x, gnext_ref, win_ref, qkv_refs, stages, row0=r.start, bw=bw, dils=dils)


def _ffn_call(x, p, layer, gffn, w1, w2, gple, wpg, wpp, gnext, win, dils, bw):
    b, s, d = x.shape
    fh = w2.shape[1]
    final = win is None
    ts = min(TS_FFN, s)
    tile = lambda w: pl.BlockSpec((1, ts, w), lambda i, j: (i, j, 0))
    in_specs = [tile(d),
                pl.BlockSpec((None, 1, ts, p.shape[-1]), lambda i, j: (layer, i, j, 0))]
    in_specs += [_layer_spec(c, lyr) for c, lyr in ((gffn, layer), (w1, 0), (w2, 0),
                                                     (gple, layer), (wpg, 0), (wpp, 0))]
    out_shape = [jax.ShapeDtypeStruct((b, s, d), F32)]
    out_specs = [tile(d)]
    scratch = [pltpu.VMEM((ts, fh), BF16)]
    if final:
        consts = (gnext,)
        in_specs.append(_layer_spec(gnext, 0))
    else:
        consts = (gnext, win)
        in_specs += [_layer_spec(gnext, layer + 1), _layer_spec(win, 0)]
        qs, qspecs, qscratch = _qkv_outs(b, s, ts, bw, dils)
        out_shape += qs
        out_specs += qspecs
        scratch = qscratch + scratch
    outs = pl.pallas_call(
        functools.partial(_ffn_body, ts=ts, fh=fh, bw=bw, dils=dils, final=final),
        out_shape=tuple(out_shape),
        grid=(b, s // ts),
        in_specs=in_specs,
        out_specs=tuple(out_specs),
        scratch_shapes=scratch,
        compiler_params=_params(2),
        name="ffn_ple",
    )(x, p, gffn, w1, w2, gple, wpg, wpp, *consts)
    return outs[0], outs[1:]


def kernel(x, p, g_mix, w_in, conv_a, sgu_ln_g, sgu_ln_b, sgu_w, sgu_b, conf_dw,
           conf_ln_g, conf_ln_b, w_branch, w_merge_gate, w_out, g_ffn, w_ffn_in,
           w_ffn_out, g_ple, w_ple_gate, w_ple_proj, g_final):
    depth = w_in.shape[0]
    d = x.shape[-1]
    bw = d // N_BRANCH
    assert all(w // dil == ATT_BLK for w, dil in DSW_GROUPS)
    dils = tuple(dil for _, dil in DSW_GROUPS)
    blk = ATT_BLK
    head_of_col = jnp.arange(bw, dtype=jnp.int32) // (bw // ATT_HEADS)
    head_of_row = jnp.arange(ATT_HEADS * blk, dtype=jnp.int32) // blk
    hm = (head_of_row[:, None] == head_of_col[None, :]).astype(BF16)
    qi = (jnp.arange(ATT_HEADS * blk, dtype=jnp.int32) % blk)[:, None]
    ki = jnp.arange(2 * blk, dtype=jnp.int32)[None, :]
    bias = jnp.where((ki >= qi) & (ki <= qi + blk), 0.0, NEG).astype(F32)

    rows = lambda a: a.reshape(a.shape[0], 1, a.shape[-1])
    g_mix, sgu_ln_g, sgu_ln_b, conf_ln_g, conf_ln_b, g_ffn, g_ple = map(
        rows, (g_mix, sgu_ln_g, sgu_ln_b, conf_ln_g, conf_ln_b, g_ffn, g_ple))
    g_final = g_final.reshape(1, 1, d)
    sgu_w = sgu_w.reshape(depth, SGU_GROUPS * SGU_CHUNK, SGU_CHUNK)
    sgubias = jnp.repeat(jnp.swapaxes(sgu_b, 1, 2), bw // SGU_GROUPS, axis=2)
    w_f32 = (w_in, w_merge_gate, w_branch, w_out, w_ffn_in, w_ffn_out, w_ple_gate, w_ple_proj)
    shapes = [(1,) + w.shape[1:] for w in w_f32]
    w_f32 = [w.reshape(depth, -1, w.shape[-1]) for w in w_f32]
    win = w_f32[0][0:1].astype(BF16)

    qkv = _qkv_call(x, g_mix, win, 0, dils, bw)
    for i in range(depth):
        final = i == depth - 1
        casts = [(w, i) for w in w_f32[1:]] + ([] if final else [(w_f32[0], i + 1)])
        y_b, cast = _attn_call(qkv, hm, bias, dils, bw, casts)
        wgate, wbr, wout, w1, w2, wpg, wpp = (
            w.reshape(sh) for w, sh in zip(cast, shapes[1:]))
        x = _mix_call(x, y_b, i, g_mix, win, conv_a, sgu_ln_g, sgu_ln_b, sgu_w, sgubias,
                      conf_dw, conf_ln_g, conf_ln_b, wgate, wbr, wout)
        win = None if final else cast[-1].reshape(shapes[0])
        x, qkv = _ffn_call(x, p, i, g_ffn, w1, w2, g_ple, wpg, wpp,
                           g_final if final else g_mix, win, dils, bw)
    return x
```

```python
import functools

import jax
import jax.numpy as jnp
from jax import lax
from jax.experimental import pallas as pl
from jax.experimental.pallas import tpu as pltpu

F32 = jnp.float32
BF16 = jnp.bfloat16

EPS = 1e-6
N_BRANCH = 4
ATT_HEADS = 4
DSW_GROUPS = ((128, 1), (512, 4), (2048, 16))
ATT_BLK = 128
ATT_SLOTS = 8
SGU_CHUNK = 128
SGU_GROUPS = 4
LANES = 128
SUBLANES = 8
BF16_ROWS = 16
HALO_A = 8
HALO_D = 32
CONV_ROWS = 64
MERGE_ROWS = 256
FFN_CHUNK = 256
FFN_SUB = 2
QKV_SUB = 2
NEG = -float("inf")
LOG2E = 1.4426950408889634
VMEM_LIMIT = 56 * 1024 * 1024

TS_QKV = 1024
TS_MIX = 512
TS_FFN = 512


def _rms(x, g):
    ms = jnp.mean(x * x, axis=-1, keepdims=True)
    return x * lax.rsqrt(ms + EPS) * g


def _ln(x, g, b):
    mu = jnp.mean(x, axis=-1, keepdims=True)
    xc = x - mu
    var = jnp.mean(xc * xc, axis=-1, keepdims=True)
    return xc * lax.rsqrt(var + EPS) * g + b


def _sigmoid(x):
    return 1.0 / (1.0 + jnp.exp2(x * (-LOG2E)))


def _dot(a, b):
    return jnp.dot(a, b, preferred_element_type=F32)


def _after(x, anchor):
    z = pltpu.bitcast(anchor[-SUBLANES:, -LANES:], jnp.uint32)
    z = lax.shift_right_logical(lax.shift_right_logical(z, jnp.uint32(16)), jnp.uint32(16))
    never = jnp.tile(z, (x.shape[0] // SUBLANES, x.shape[1] // LANES)) != 0
    return jnp.where(never, jnp.zeros_like(x), x)


def _const_spec(shape):
    nd = len(shape)
    return pl.BlockSpec(shape, lambda *_: (0,) * nd, pipeline_mode=pl.Buffered(1))


def _layer_spec(arr, layer):
    nd = arr.ndim - 1
    return pl.BlockSpec((None,) + arr.shape[1:], lambda *_: (layer,) + (0,) * nd,
                        pipeline_mode=pl.Buffered(1))


def _cast_rows(r, steps):
    br = BF16_ROWS * pl.cdiv(pl.cdiv(r, steps), BF16_ROWS)
    while r % br:
        br += BF16_ROWS
    return br


def _params(n_grid):
    return pltpu.CompilerParams(
        dimension_semantics=("arbitrary",) * n_grid,
        vmem_limit_bytes=VMEM_LIMIT)


def _emit_qkv(x, g_ref, win_ref, outs, stages, *, row0, bw, dils):
    rs = x.shape[0]
    h = _rms(x, g_ref[...]).astype(BF16)
    qkv = _dot(h, win_ref[:, 3 * bw:6 * bw])
    nsl = bw // LANES
    scale = float((bw // ATT_HEADS) ** -0.5 * LOG2E)
    for n, dil in enumerate(dils):
        last = n == len(dils) - 1
        dst = slice(row0 // dil, (row0 + rs) // dil)
        for c in range(3 * nsl):
            t, sl = divmod(c, nsl)
            for r in range(dil):
                if n == 0:
                    rows = qkv[:, c * LANES:(c + 1) * LANES]
                    rows = rows * scale if t == 0 else rows
                else:
                    prev = dils[n - 1]
                    rows = stages[n - 1][c, r % prev,
                                         pl.ds(row0 // prev + r // prev, rs // dil,
                                               stride=dil // prev), :]
                if not last:
                    stages[n][c, r, dst, :] = rows
                lo = r * bw + sl * LANES
                outs[3 * n + t][0, dst, lo:lo + LANES] = rows.astype(BF16)


def _qkv_outs(b, s, ts, bw, dils):
    assert all(b_ % a_ == 0 for a_, b_ in zip(dils, dils[1:])) and dils[0] == 1
    out_shape, out_specs = [], []
    for dil in dils:
        for _ in range(3):
            out_shape.append(jax.ShapeDtypeStruct((b, s // dil, dil * bw), BF16))
            out_specs.append(pl.BlockSpec((1, ts // dil, dil * bw), lambda i, j: (i, j, 0)))
    scratch = [pltpu.VMEM((3 * bw // LANES, dil, ts // dil, LANES), F32) for dil in dils[:-1]]
    return out_shape, out_specs, scratch


def _qkv_body(x_ref, g_ref, win_ref, *rest, ts, bw, dils):
    n_out = 3 * len(dils)
    rs = ts // QKV_SUB
    for h in range(QKV_SUB):
        _emit_qkv(x_ref[0, h * rs:(h + 1) * rs, :], g_ref, win_ref, rest[:n_out], rest[n_out:],
                  row0=h * rs, bw=bw, dils=dils)


def _qkv_call(x, g, win, layer, dils, bw):
    b, s, d = x.shape
    ts = min(TS_QKV, s)
    out_shape, out_specs, scratch = _qkv_outs(b, s, ts, bw, dils)
    return pl.pallas_call(
        functools.partial(_qkv_body, ts=ts, bw=bw, dils=dils),
        out_shape=tuple(out_shape),
        grid=(b, s // ts),
        in_specs=[pl.BlockSpec((1, ts, d), lambda i, j: (i, j, 0)),
                  _layer_spec(g, layer), _layer_spec(win, 0)],
        out_specs=tuple(out_specs),
        scratch_shapes=scratch,
        compiler_params=_params(2),
        name="qkv_proj",
    )(x, g, win)


def _attn_block(q, kk, vv, hm, bias, first, bw):
    blk = ATT_BLK
    nh = ATT_HEADS
    dh = bw // nh
    hps = LANES // dh
    qst = jnp.concatenate([q] * nh, axis=0) * hm
    s = lax.dot_general(qst, kk, (((1,), (1,)), ((), ())), preferred_element_type=F32)
    s = s + bias
    if first is not False:
        s = jnp.concatenate([jnp.where(first, NEG, s[:, :blk]), s[:, blk:]], axis=1)
    m = jnp.max(s, axis=1, keepdims=True)
    p = jnp.exp2(s - m).astype(BF16)
    ones = jnp.ones((2 * blk, LANES), BF16)
    head_in_slab = lax.broadcasted_iota(jnp.int32, (blk, LANES), 1) // dh
    out = []
    for sl in range(bw // LANES):
        r0 = sl * hps * blk
        rhs = jnp.concatenate([vv[:, sl * LANES:(sl + 1) * LANES], ones], axis=1)
        ol = _dot(p[r0:r0 + hps * blk], rhs)
        o_s, l_s = ol[0:blk, :LANES], ol[0:blk, LANES:]
        m_s = jnp.broadcast_to(m[r0:r0 + blk], (blk, LANES))
        for hh in range(1, hps):
            sel = head_in_slab == hh
            rows = slice(hh * blk, (hh + 1) * blk)
            o_s = jnp.where(sel, ol[rows, :LANES], o_s)
            l_s = jnp.where(sel, ol[rows, LANES:], l_s)
            m_s = jnp.where(sel, m[r0 + hh * blk:r0 + (hh + 1) * blk], m_s)
        out.append((o_s, m_s, l_s))
    return out


def _attn_tiling(dil, span):
    rows = max(ATT_SLOTS // dil, 1)
    cols = min(dil, ATT_SLOTS)
    row_steps = span // (ATT_BLK * dil) // rows
    col_steps = dil // cols
    return rows, cols, row_steps, col_steps


def _attn_body(*refs, bw, dils, span, n_cast):
    ng = len(dils)
    hm_ref, bias_ref = refs[5 * ng:5 * ng + 2]
    cast_in = refs[5 * ng + 2:5 * ng + 2 + n_cast]
    yb_ref = refs[5 * ng + 2 + n_cast]
    cast_out = refs[5 * ng + 3 + n_cast:5 * ng + 3 + 2 * n_cast]
    onat, mnat, lnat = refs[5 * ng + 3 + 2 * n_cast:]
    for src, dst in zip(cast_in, cast_out):
        dst[...] = src[...].astype(BF16)
    blk = ATT_BLK
    nsl = bw // LANES
    j = pl.program_id(1)
    it = pl.program_id(2)
    hm = hm_ref[...]
    bias = bias_ref[...]
    for g, dil in enumerate(dils):
        q_ref, kc_ref, kp_ref, vc_ref, vp_ref = refs[5 * g:5 * g + 5]
        rows, cols, _, col_steps = _attn_tiling(dil, span)
        row_step = it // col_steps if col_steps > 1 else it
        col_step = it % col_steps if col_steps > 1 else 0
        for a in range(rows):
            for c in range(cols):
                cs = slice(c * bw, (c + 1) * bw)
                q = q_ref[0, a * blk:(a + 1) * blk, cs]
                if a == 0:
                    kk = jnp.concatenate([kp_ref[0, :, cs], kc_ref[0, 0:blk, cs]], axis=0)
                    vv = jnp.concatenate([vp_ref[0, :, cs], vc_ref[0, 0:blk, cs]], axis=0)
                    first = (j == 0) & (row_step == 0)
                else:
                    kk = kc_ref[0, (a - 1) * blk:(a + 1) * blk, cs]
                    vv = vc_ref[0, (a - 1) * blk:(a + 1) * blk, cs]
                    first = False
                stats = _attn_block(q, kk, vv, hm, bias, first, bw)
                start = (row_step * rows + a) * (blk * dil) + col_step * cols + c
                if dil == 1:
                    idx = pl.ds(pl.multiple_of(start, blk), blk)
                else:
                    idx = pl.ds(start, blk, stride=dil)
                for sl, (o_s, m_s, l_s) in enumerate(stats):
                    onat[g, sl, idx, :] = o_s
                    mnat[g, sl, idx, :] = m_s
                    lnat[g, sl, idx, :] = l_s

    @pl.when(it == pl.num_programs(2) - 1)
    def _():
        def merge(c, carry):
            rows = pl.ds(pl.multiple_of(c * MERGE_ROWS, MERGE_ROWS), MERGE_ROWS)
            for sl in range(nsl):
                ms = [mnat[g, sl, rows, :] for g in range(ng)]
                mm = functools.reduce(jnp.maximum, ms)
                ws = [jnp.exp2(m - mm) for m in ms]
                num = sum(w * onat[g, sl, rows, :] for g, w in enumerate(ws))
                den = sum(w * lnat[g, sl, rows, :] for g, w in enumerate(ws))
                yb_ref[0, rows, sl * LANES:(sl + 1) * LANES] = (num / den).astype(BF16)
            return carry
        lax.fori_loop(0, span // MERGE_ROWS, merge, 0)


def _attn_call(qkv, hm, bias, dils, bw, casts):
    b, s, _ = qkv[0].shape
    blk = ATT_BLK
    span = blk * max(dils)
    nsteps = span // blk // ATT_SLOTS
    in_specs, args = [], []
    for g, dil in enumerate(dils):
        rows, cols, row_steps, col_steps = _attn_tiling(dil, span)
        assert row_steps * col_steps == nsteps

        def cur(i, j, it, rs=row_steps, cst=col_steps):
            return (i, j * rs + it // cst, it % cst)

        def prev(i, j, it, rs=row_steps, cst=col_steps, r=rows):
            return (i, jnp.maximum((j * rs + it // cst) * r - 1, 0), it % cst)

        q, k, v = qkv[3 * g:3 * g + 3]
        cur_spec = pl.BlockSpec((1, rows * blk, cols * bw), cur)
        prev_spec = pl.BlockSpec((1, blk, cols * bw), prev)
        in_specs += [cur_spec, cur_spec, prev_spec, cur_spec, prev_spec]
        args += [q, k, k, v, v]
    nsl = bw // LANES
    n_tiles = s // span
    out_shape = [jax.ShapeDtypeStruct((b, s, bw), BF16)]
    out_specs = [pl.BlockSpec((1, span, bw), lambda i, j, u: (i, j, 0))]
    cast_specs = []
    for w, layer in casts:
        _, r, c = w.shape
        br = _cast_rows(r, b * n_tiles * nsteps)
        at = lambda i, j, u, last=r // br - 1: jnp.minimum((i * n_tiles + j) * nsteps + u, last)
        cast_specs.append(pl.BlockSpec((None, br, c),
                                       lambda i, j, u, at=at, lyr=layer: (lyr, at(i, j, u), 0)))
        out_specs.append(pl.BlockSpec((br, c), lambda i, j, u, at=at: (at(i, j, u), 0)))
        out_shape.append(jax.ShapeDtypeStruct((r, c), BF16))
    outs = pl.pallas_call(
        functools.partial(_attn_body, bw=bw, dils=dils, span=span, n_cast=len(casts)),
        out_shape=tuple(out_shape),
        grid=(b, n_tiles, nsteps),
        in_specs=in_specs + [_const_spec(hm.shape), _const_spec(bias.shape)] + cast_specs,
        out_specs=tuple(out_specs),
        scratch_shapes=[pltpu.VMEM((len(dils), nsl, span, LANES), F32)] * 3,
        compiler_params=_params(3),
        name="band_attn",
    )(*args, hm, bias, *[w for w, _ in casts])
    return outs[0], outs[1:]


def _mix_body(x_ref, yb_ref,
              gmix_ref, win_ref, conva_ref, sgug_ref, sgub_ref, sguw_ref, sgubias_ref,
              confw_ref, confg_ref, confb_ref, wgate_ref, wbr_ref, wout_ref,
              xo_ref, exta, extd, *, ts, bw, kd):
    nslab = bw // LANES

    @pl.when(pl.program_id(1) == 0)
    def _():
        exta[:, 0:HALO_A, :] = jnp.zeros((nslab, HALO_A, LANES), F32)
        extd[:, 0:HALO_D, :] = jnp.zeros((nslab, HALO_D, LANES), F32)

    x = x_ref[0]
    hb = _rms(x, gmix_ref[...]).astype(BF16)
    a_b, a_c, a_x = (_dot(hb, win_ref[:, n * bw:(n + 1) * bw]) for n in range(3))
    s_u, s_v, c_val, c_gate = (_dot(hb, win_ref[:, n * bw:(n + 1) * bw]) for n in range(6, 10))
    gates = [_sigmoid(_dot(hb, wgate_ref[n])) for n in range(N_BRANCH)]

    ca = a_c * a_x
    ka = conva_ref.shape[0]
    ya = []
    for sl in range(nslab):
        ls = slice(sl * LANES, (sl + 1) * LANES)
        exta[sl, HALO_A:HALO_A + ts, :] = ca[:, ls]
        acc = None
        for t in range(ka):
            off = HALO_A - (ka - 1) + t
            term = conva_ref[t:t + 1, ls] * exta[sl, off:off + ts, :]
            acc = term if acc is None else acc + term
        ya.append(a_b[:, ls] * acc)
        exta[sl, 0:HALO_A, :] = exta[sl, ts:ts + HALO_A, :]
    y_a = jnp.concatenate(ya, axis=1)

    vb = _ln(s_v, sgug_ref[...], sgub_ref[...]).astype(BF16)
    ck = SGU_CHUNK
    wr = lax.broadcasted_iota(jnp.int32, (SGU_GROUPS * ck, ck), 0) & (ck - 1)
    wc = lax.broadcasted_iota(jnp.int32, (SGU_GROUPS * ck, ck), 1)
    wst = jnp.where(wc <= wr, sguw_ref[...], 0.0).astype(BF16)
    group_of_lane = lax.broadcasted_iota(jnp.int32, (ck, bw), 1) // (bw // SGU_GROUPS)
    yc = []
    for c in range(ts // ck):
        mm = _dot(wst, vb[c * ck:(c + 1) * ck, :])
        mixed = mm[0:ck]
        for g in range(1, SGU_GROUPS):
            mixed = jnp.where(group_of_lane == g, mm[g * ck:(g + 1) * ck], mixed)
        yc.append(s_u[c * ck:(c + 1) * ck, :] * (mixed + sgubias_ref[...]))
    y_c = jnp.concatenate(yc, axis=0)

    glu = c_val * _sigmoid(c_gate)
    for sl in range(nslab):
        extd[sl, HALO_D:HALO_D + ts, :] = glu[:, sl * LANES:(sl + 1) * LANES]
    cd = []
    n_pieces = nslab * (ts // CONV_ROWS)
    for sl in range(nslab):
        ls = slice(sl * LANES, (sl + 1) * LANES)
        pieces = []
        for r0 in range(0, ts, CONV_ROWS):
            acc = None
            for t in range(kd):
                off = r0 + HALO_D - (kd - 1) + t
                term = confw_ref[t:t + 1, ls] * extd[sl, off:off + CONV_ROWS, :]
                acc = term if acc is None else acc + term
            pi = len(pieces) + sl * (ts // CONV_ROWS)
            if pi % (n_pieces // N_BRANCH) == 0 and pi > 0:
                acc = _after(acc, gates[pi // (n_pieces // N_BRANCH) - 1])
            pieces.append(acc)
        cd.append(jnp.concatenate(pieces, axis=0))
        extd[sl, 0:HALO_D, :] = extd[sl, ts:ts + HALO_D, :]
    cd = _ln(jnp.concatenate(cd, axis=1), confg_ref[...], confb_ref[...])
    y_d = cd * _sigmoid(cd)

    ys = [y.astype(BF16) for y in (y_a, yb_ref[0], y_c, y_d)]
    slabs = []
    for c0 in range(0, x.shape[1], 256):
        merged = None
        for n, y in enumerate(ys):
            term = gates[n][:, c0:c0 + 256] * _dot(y, wbr_ref[n, :, c0:c0 + 256])
            merged = term if merged is None else merged + term
        slabs.append(merged.astype(BF16))
    xo_ref[0] = x + _dot(jnp.concatenate(slabs, axis=1), wout_ref[...])


def _mix_call(x, y_b, layer, gmix, win, conva, sgug, sgub, sguw, sgubias,
              confw, confg, confb, wgate, wbr, wout):
    b, s, d = x.shape
    bw = conva.shape[-1]
    kd = confw.shape[1]
    ts = min(TS_MIX, s)
    nslab = bw // LANES
    tile = lambda w: pl.BlockSpec((1, ts, w), lambda i, j: (i, j, 0))
    consts = (gmix, win, conva, sgug, sgub, sguw, sgubias, confw, confg, confb,
              wgate, wbr, wout)
    own = (win, wgate, wbr, wout)
    return pl.pallas_call(
        functools.partial(_mix_body, ts=ts, bw=bw, kd=kd),
        out_shape=jax.ShapeDtypeStruct((b, s, d), F32),
        grid=(b, s // ts),
        in_specs=[tile(d), tile(bw)] + [_layer_spec(c, 0 if any(c is o for o in own) else layer) for c in consts],
        out_specs=tile(d),
        scratch_shapes=[pltpu.VMEM((nslab, HALO_A + ts, LANES), F32),
                        pltpu.VMEM((nslab, HALO_D + ts, LANES), F32)],
        compiler_params=_params(2),
        name="mix_merge",
    )(x, y_b, *consts)


def _ffn_body(x_ref, p_ref, gffn_ref, w1_ref, w2_ref, gple_ref, wpg_ref, wpp_ref,
              gnext_ref, *rest, ts, fh, bw, dils, final):
    if final:
        xo_ref, hid_ref = rest
    else:
        n_out = 3 * len(dils)
        win_ref, xo_ref = rest[:2]
        qkv_refs, stages, hid_ref = rest[2:2 + n_out], rest[2 + n_out:-1], rest[-1]
    rs = ts // FFN_SUB
    subs = [slice(h * rs, (h + 1) * rs) for h in range(FFN_SUB)]
    xs = [x_ref[0, r, :] for r in subs]
    hbs = [_rms(x, gffn_ref[...]).astype(BF16) for x in xs]
    for c in range(fh // FFN_CHUNK):
        c0 = c * FFN_CHUNK
        for r, hb in zip(subs, hbs):
            gate = _dot(hb, w1_ref[:, c0:c0 + FFN_CHUNK])
            up = _dot(hb, w1_ref[:, fh + c0:fh + c0 + FFN_CHUNK])
            hid_ref[r, c0:c0 + FFN_CHUNK] = (gate * _sigmoid(gate) * up).astype(BF16)
    xs = [x + _dot(hid_ref[r, :], w2_ref[...]) for r, x in zip(subs, xs)]
    h3s = [_rms(x, gple_ref[...]).astype(BF16) for x in xs]
    pbs = [p_ref[0, r, :].astype(BF16) for r in subs]
    xs = [jnp.concatenate(
        [x[:, c0:c0 + FFN_CHUNK] + _sigmoid(_dot(h3, wpg_ref[:, c0:c0 + FFN_CHUNK]))
         * _dot(pb, wpp_ref[:, c0:c0 + FFN_CHUNK]) for c0 in range(0, x.shape[1], FFN_CHUNK)],
        axis=1) for x, h3, pb in zip(xs, h3s, pbs)]
    for r, x in zip(subs, xs):
        if final:
            xo_ref[0, r, :] = _rms(x, gnext_ref[...])
        else:
            xo_ref[0, r, :] = x
            _emit_qkv(x, gnext_ref, win_ref, qkv_refs, stages, row0=r.start, bw=bw, dils=dils)


def _ffn_call(x, p, layer, gffn, w1, w2, gple, wpg, wpp, gnext, win, dils, bw):
    b, s, d = x.shape
    fh = w2.shape[1]
    final = win is None
    ts = min(TS_FFN, s)
    tile = lambda w: pl.BlockSpec((1, ts, w), lambda i, j: (i, j, 0))
    in_specs = [tile(d),
                pl.BlockSpec((None, 1, ts, p.shape[-1]), lambda i, j: (layer, i, j, 0))]
    in_specs += [_layer_spec(c, lyr) for c, lyr in ((gffn, layer), (w1, 0), (w2, 0),
                                                     (gple, layer), (wpg, 0), (wpp, 0))]
    out_shape = [jax.ShapeDtypeStruct((b, s, d), F32)]
    out_specs = [tile(d)]
    scratch = [pltpu.VMEM((ts, fh), BF16)]
    if final:
        consts = (gnext,)
        in_specs.append(_layer_spec(gnext, 0))
    else:
        consts = (gnext, win)
        in_specs += [_layer_spec(gnext, layer + 1), _layer_spec(win, 0)]
        qs, qspecs, qscratch = _qkv_outs(b, s, ts, bw, dils)
        out_shape += qs
        out_specs += qspecs
        scratch = qscratch + scratch
    outs = pl.pallas_call(
        functools.partial(_ffn_body, ts=ts, fh=fh, bw=bw, dils=dils, final=final),
        out_shape=tuple(out_shape),
        grid=(b, s // ts),
        in_specs=in_specs,
        out_specs=tuple(out_specs),
        scratch_shapes=scratch,
        compiler_params=_params(2),
        name="ffn_ple",
    )(x, p, gffn, w1, w2, gple, wpg, wpp, *consts)
    return outs[0], outs[1:]


def kernel(x, p, g_mix, w_in, conv_a, sgu_ln_g, sgu_ln_b, sgu_w, sgu_b, conf_dw,
           conf_ln_g, conf_ln_b, w_branch, w_merge_gate, w_out, g_ffn, w_ffn_in,
           w_ffn_out, g_ple, w_ple_gate, w_ple_proj, g_final):
    depth = w_in.shape[0]
    d = x.shape[-1]
    bw = d // N_BRANCH
    assert all(w // dil == ATT_BLK for w, dil in DSW_GROUPS)
    dils = tuple(dil for _, dil in DSW_GROUPS)
    blk = ATT_BLK
    head_of_col = jnp.arange(bw, dtype=jnp.int32) // (bw // ATT_HEADS)
    head_of_row = jnp.arange(ATT_HEADS * blk, dtype=jnp.int32) // blk
    hm = (head_of_row[:, None] == head_of_col[None, :]).astype(BF16)
    qi = (jnp.arange(ATT_HEADS * blk, dtype=jnp.int32) % blk)[:, None]
    ki = jnp.arange(2 * blk, dtype=jnp.int32)[None, :]
    bias = jnp.where((ki >= qi) & (ki <= qi + blk), 0.0, NEG).astype(F32)

    rows = lambda a: a.reshape(a.shape[0], 1, a.shape[-1])
    g_mix, sgu_ln_g, sgu_ln_b, conf_ln_g, conf_ln_b, g_ffn, g_ple = map(
        rows, (g_mix, sgu_ln_g, sgu_ln_b, conf_ln_g, conf_ln_b, g_ffn, g_ple))
    g_final = g_final.reshape(1, 1, d)
    sgu_w = sgu_w.reshape(depth, SGU_GROUPS * SGU_CHUNK, SGU_CHUNK)
    sgubias = jnp.repeat(jnp.swapaxes(sgu_b, 1, 2), bw // SGU_GROUPS, axis=2)
    w_f32 = (w_in, w_merge_gate, w_branch, w_out, w_ffn_in, w_ffn_out, w_ple_gate, w_ple_proj)
    shapes = [(1,) + w.shape[1:] for w in w_f32]
    w_f32 = [w.reshape(depth, -1, w.shape[-1]) for w in w_f32]
    win = w_f32[0][0:1].astype(BF16)

    qkv = _qkv_call(x, g_mix, win, 0, dils, bw)
    for i in range(depth):
        final = i == depth - 1
        casts = [(w, i) for w in w_f32[1:]] + ([] if final else [(w_f32[0], i + 1)])
        y_b, cast = _attn_call(qkv, hm, bias, dils, bw, casts)
        wgate, wbr, wout, w1, w2, wpg, wpp = (
            w.reshape(sh) for w, sh in zip(cast, shapes[1:]))
        x = _mix_call(x, y_b, i, g_mix, win, conv_a, sgu_ln_g, sgu_ln_b, sgu_w, sgubias,
                      conf_dw, conf_ln_g, conf_ln_b, wgate, wbr, wout)
        win = None if final else cast[-1].reshape(shapes[0])
        x, qkv = _ffn_call(x, p, i, g_ffn, w1, w2, g_ple, wpg, wpp,
                           g_final if final else g_mix, win, dils, bw)
    return x
```
